```python
import jax, jax.numpy as jnp
from jax import lax
import numpy as np

D_MODEL = 1024
BATCH = 8
SEQ = 2048
DEPTH = 4
DEC_BATCH = 128
DEC_SEQ = 4
PAST_LEN = 16384
PAGE_SIZE = 128

N_A_LAYERS = (DEPTH + 1) // 2
N_B_LAYERS = DEPTH // 2
N_VRES = max(N_B_LAYERS - 1, 0)
CHUNK = 128
A_WIDTH = 2 * D_MODEL
A_GROUPS = 8
A_GROUP_DIM = A_WIDTH // A_GROUPS
HEAD_DIM = 64
N_HEADS = D_MODEL // HEAD_DIM
LORA_W = 64
LORA_A = 64
LORA_V = 32
LORA_G = 160
D_FF = 2816
N_EXPERTS = 8
TOP_K = 2
RMS_EPS = 1e-6
LN_EPS = 1e-5
GN_EPS = 64e-5

kernel_name = "gmlp_chunk_rwkv7_hybrid_step"


def _rmsnorm(x, g):
    xf = x.astype(jnp.float32)
    y = xf * lax.rsqrt(jnp.mean(xf * xf, axis=-1, keepdims=True) + RMS_EPS)
    return (y * g.astype(jnp.float32)).astype(x.dtype)


def _layernorm(x, g, b):
    xf = x.astype(jnp.float32)
    xc = xf - jnp.mean(xf, axis=-1, keepdims=True)
    y = xc * lax.rsqrt(jnp.mean(xc * xc, axis=-1, keepdims=True) + LN_EPS)
    return (y * g.astype(jnp.float32) + b.astype(jnp.float32)).astype(x.dtype)


def _chunk_sgu_mix(h, w_in, b_in, ln_g, ln_b, w_s, b_s, w_out):
    bsz, t, _ = h.shape
    L = min(t, CHUNK)
    n_chunks = t // L
    z = jax.nn.gelu(h @ w_in + b_in, approximate=False)
    u, v = jnp.split(z, 2, axis=-1)
    v = _layernorm(v, ln_g, ln_b)
    mask = jnp.tril(jnp.ones((L, L), dtype=bool))
    w_c = jnp.where(mask, w_s[:, :L, :L], 0).astype(v.dtype)
    vc = v.reshape(bsz, n_chunks, L, A_GROUPS, A_GROUP_DIM)
    mixed = jnp.einsum("gts,bcsgd->bctgd", w_c, vc) + b_s[:, :L].T[:, :, None].astype(v.dtype)
    out = u * mixed.reshape(bsz, t, A_WIDTH)
    return out @ w_out, v


def _wkv7_scan(r, decay, k, v, kk, alpha, s0):
    def step(s, inp):
        r_t, w_t, k_t, v_t, kk_t, a_t = inp
        s_kk = jnp.einsum("bhij,bhj->bhi", s, kk_t)
        s = (s * w_t[:, :, None, :]
             - s_kk[..., None] * (kk_t * a_t)[:, :, None, :]
             + v_t[..., None] * k_t[:, :, None, :])
        y = jnp.einsum("bhij,bhj->bhi", s, r_t)
        return s, y
    xs = tuple(jnp.swapaxes(z, 0, 1) for z in (r, decay, k, v, kk, alpha))
    s_last, ys = lax.scan(step, s0, xs)
    return jnp.swapaxes(ys, 0, 1), s_last


def _rwkv7_mix(h, shift0, wkv0, v_first, vres, mu, w_rkv, w0, w1, w2, a0, a1, a2,
               g1, g2, k_k, k_a, r_k, lnx_g, lnx_b, w_out):
    bsz, t, _ = h.shape
    h_prev = jnp.concatenate([shift0[:, None, :].astype(h.dtype), h[:, :-1]], axis=1)
    xx = h_prev - h
    mixed = h[None] + xx[None] * mu[:, None, None, :]
    r, k, v = jnp.einsum("sbtd,sde->sbte", mixed[:3], w_rkv)
    xv, xw, xa, xg = mixed[2], mixed[3], mixed[4], mixed[5]
    w_log = -jax.nn.softplus(-(w0 + jnp.tanh(xw @ w1) @ w2)) - 0.5
    decay = jnp.exp(-jnp.exp(w_log.astype(jnp.float32)))
    if vres is None:
        v_first = v
    else:
        v0, v1, v2 = vres
        v = v + (v_first - v) * jax.nn.sigmoid(v0 + (xv @ v1) @ v2)
    alpha = jax.nn.sigmoid(a0 + (xa @ a1) @ a2)
    g = jax.nn.sigmoid(xg @ g1) @ g2

    def heads(z):
        return z.reshape(bsz, t, N_HEADS, HEAD_DIM).astype(jnp.float32)

    kk = heads(k * k_k)
    kk = kk / jnp.maximum(jnp.sqrt(jnp.sum(kk * kk, axis=-1, keepdims=True)), 1e-12)
    k = k * (1 + (alpha - 1) * k_a)
    rh, kh, vh, ah, wh = heads(r), heads(k), heads(v), heads(alpha), heads(decay)
    y, s_last = _wkv7_scan(rh, wh, kh, vh, kk, ah, wkv0.astype(jnp.float32))
    yc = y - jnp.mean(y, axis=-1, keepdims=True)
    y = yc * lax.rsqrt(jnp.mean(yc * yc, axis=-1, keepdims=True) + GN_EPS)
    y = y.reshape(bsz, t, D_MODEL) * lnx_g.astype(jnp.float32) + lnx_b.astype(jnp.float32)
    bonus = jnp.sum(rh * kh * r_k.astype(jnp.float32), axis=-1, keepdims=True) * vh
    y = (y + bonus.reshape(bsz, t, D_MODEL)).astype(h.dtype)
    out = (y * g) @ w_out
    return out, h[:, -1].astype(shift0.dtype), s_last.astype(wkv0.dtype), v_first


def _swiglu(h, wg, wu, wd):
    return (jax.nn.silu(h @ wg) * (h @ wu)) @ wd


def _moe(h, router, wg, wu, wd):
    shp = h.shape
    xt = h.reshape(-1, D_MODEL)
    logits = (xt @ router).astype(jnp.float32)
    top_v, top_i = lax.top_k(logits, TOP_K)
    top_w = jax.nn.softmax(top_v, axis=-1)
    gates = jnp.sum(jax.nn.one_hot(top_i, N_EXPERTS, dtype=jnp.float32) * top_w[..., None],
                    axis=1).astype(h.dtype)
    y = jnp.zeros_like(xt)
    for e in range(N_EXPERTS):
        y = y + gates[:, e:e + 1] * _swiglu(xt, wg[e], wu[e], wd[e])
    return y.reshape(shp)


def _trunk(x, shift_in, wkv_in, p):
    chunk_v, shift_out, wkv_out = [], [], []
    v_first = None
    for i in range(DEPTH):
        j = i // 2
        h = _rmsnorm(x, p["norm_mix_g"][i])
        if i % 2 == 0:
            y, v_rows = _chunk_sgu_mix(h, p["a_w_in"][j], p["a_b_in"][j], p["a_ln_g"][j],
                                       p["a_ln_b"][j], p["a_w_s"][j], p["a_b_s"][j],
                                       p["a_w_out"][j])
            chunk_v.append(v_rows)
        else:
            vres = None if j == 0 else (p["b_v0"][j - 1], p["b_v1"][j - 1], p["b_v2"][j - 1])
            y, s_row, s_wkv, v_first = _rwkv7_mix(
                h, shift_in[j], wkv_in[j], v_first, vres, p["b_mu"][j], p["b_w_rkv"][j],
                p["b_w0"][j], p["b_w1"][j], p["b_w2"][j], p["b_a0"][j], p["b_a1"][j],
                p["b_a2"][j], p["b_g1"][j], p["b_g2"][j], p["b_k_k"][j], p["b_k_a"][j],
                p["b_r_k"][j], p["b_lnx_g"][j], p["b_lnx_b"][j], p["b_w_out"][j])
            shift_out.append(s_row)
            wkv_out.append(s_wkv)
        x = x + y
        h = _rmsnorm(x, p["norm_ffn_g"][i])
        if i % 2 == 0:
            x = x + _swiglu(h, p["f_w_gate"][j], p["f_w_up"][j], p["f_w_down"][j])
        else:
            x = x + _moe(h, p["m_router"][j], p["m_w_gate"][j], p["m_w_up"][j], p["m_w_down"][j])
    return _rmsnorm(x, p["norm_out_g"]), chunk_v, shift_out, wkv_out


def setup_inputs(seed: int = 0) -> dict:
    key = jax.random.key(seed)
    ks = iter(jax.random.split(key, 64))
    D = D_MODEL

    def nrm(shape, scale):
        return jax.random.normal(next(ks), shape, jnp.float32) * scale

    def uni(shape, lo, hi):
        return jax.random.uniform(next(ks), shape, jnp.float32, lo, hi)

    return {
        "x_prompt": nrm((BATCH, SEQ, D), 1.0),
        "x_sample": nrm((DEC_BATCH, DEC_SEQ, D), 1.0),
        "state_shift": nrm((N_B_LAYERS, DEC_BATCH, D), 1.0),
        "state_wkv": nrm((N_B_LAYERS, DEC_BATCH, N_HEADS, HEAD_DIM, HEAD_DIM), 0.3),
        "norm_mix_g": 1.0 + nrm((DEPTH, D), 0.02),
        "norm_ffn_g": 1.0 + nrm((DEPTH, D), 0.02),
        "norm_out_g": 1.0 + nrm((D,), 0.02),
        "a_w_in": nrm((N_A_LAYERS, D, 2 * A_WIDTH), D ** -0.5),
        "a_b_in": nrm((N_A_LAYERS, 2 * A_WIDTH), 0.02),
        "a_ln_g": 1.0 + nrm((N_A_LAYERS, A_WIDTH), 0.02),
        "a_ln_b": nrm((N_A_LAYERS, A_WIDTH), 0.02),
        "a_w_s": nrm((N_A_LAYERS, A_GROUPS, CHUNK, CHUNK), 0.5 * CHUNK ** -0.5),
        "a_b_s": 1.0 + nrm((N_A_LAYERS, A_GROUPS, CHUNK), 0.1),
        "a_w_out": nrm((N_A_LAYERS, A_WIDTH, D), A_WIDTH ** -0.5),
        "b_mu": uni((N_B_LAYERS, 6, D), 0.0, 1.0),
        "b_w_rkv": nrm((N_B_LAYERS, 3, D, D), D ** -0.5),
        "b_w0": uni((N_B_LAYERS, D), -6.0, -1.0),
        "b_w1": nrm((N_B_LAYERS, D, LORA_W), D ** -0.5),
        "b_w2": nrm((N_B_LAYERS, LORA_W, D), 0.5 * LORA_W ** -0.5),
        "b_a0": nrm((N_B_LAYERS, D), 0.1),
        "b_a1": nrm((N_B_LAYERS, D, LORA_A), D ** -0.5),
        "b_a2": nrm((N_B_LAYERS, LORA_A, D), 0.5 * LORA_A ** -0.5),
        "b_v0": 1.0 + nrm((N_VRES, D), 0.1),
        "b_v1": nrm((N_VRES, D, LORA_V), D ** -0.5),
        "b_v2": nrm((N_VRES, LORA_V, D), 0.5 * LORA_V ** -0.5),
        "b_g1": nrm((N_B_LAYERS, D, LORA_G), D ** -0.5),
        "b_g2": nrm((N_B_LAYERS, LORA_G, D), LORA_G ** -0.5),
        "b_k_k": 0.85 + nrm((N_B_LAYERS, D), 0.02),
        "b_k_a": 1.0 + nrm((N_B_LAYERS, D), 0.02),
        "b_r_k": nrm((N_B_LAYERS, N_HEADS, HEAD_DIM), 0.1),
        "b_lnx_g": 1.0 + nrm((N_B_LAYERS, D), 0.02),
        "b_lnx_b": nrm((N_B_LAYERS, D), 0.02),
        "b_w_out": nrm((N_B_LAYERS, D, D), D ** -0.5),
        "f_w_gate": nrm((N_A_LAYERS, D, D_FF), D ** -0.5),
        "f_w_up": nrm((N_A_LAYERS, D, D_FF), D ** -0.5),
        "f_w_down": nrm((N_A_LAYERS, D_FF, D), D_FF ** -0.5),
        "m_router": nrm((N_B_LAYERS, D, N_EXPERTS), D ** -0.5),
        "m_w_gate": nrm((N_B_LAYERS, N_EXPERTS, D, D_FF), D ** -0.5),
        "m_w_up": nrm((N_B_LAYERS, N_EXPERTS, D, D_FF), D ** -0.5),
        "m_w_down": nrm((N_B_LAYERS, N_EXPERTS, D_FF, D), D_FF ** -0.5),
    }


def reference(x_prompt, x_sample, state_shift, state_wkv, norm_mix_g, norm_ffn_g, norm_out_g,
              a_w_in, a_b_in, a_ln_g, a_ln_b, a_w_s, a_b_s, a_w_out,
              b_mu, b_w_rkv, b_w0, b_w1, b_w2, b_a0, b_a1, b_a2, b_v0, b_v1, b_v2,
              b_g1, b_g2, b_k_k, b_k_a, b_r_k, b_lnx_g, b_lnx_b, b_w_out,
              f_w_gate, f_w_up, f_w_down, m_router, m_w_gate, m_w_up, m_w_down):
    p = dict(norm_mix_g=norm_mix_g, norm_ffn_g=norm_ffn_g, norm_out_g=norm_out_g,
             a_w_in=a_w_in, a_b_in=a_b_in, a_ln_g=a_ln_g, a_ln_b=a_ln_b, a_w_s=a_w_s,
             a_b_s=a_b_s, a_w_out=a_w_out, b_mu=b_mu, b_w_rkv=b_w_rkv, b_w0=b_w0, b_w1=b_w1,
             b_w2=b_w2, b_a0=b_a0, b_a1=b_a1, b_a2=b_a2, b_v0=b_v0, b_v1=b_v1, b_v2=b_v2,
             b_g1=b_g1, b_g2=b_g2, b_k_k=b_k_k, b_k_a=b_k_a, b_r_k=b_r_k, b_lnx_g=b_lnx_g,
             b_lnx_b=b_lnx_b, b_w_out=b_w_out, f_w_gate=f_w_gate, f_w_up=f_w_up,
             f_w_down=f_w_down, m_router=m_router, m_w_gate=m_w_gate, m_w_up=m_w_up,
             m_w_down=m_w_down)
    shift_zero = jnp.zeros((N_B_LAYERS, BATCH, D_MODEL), state_shift.dtype)
    wkv_zero = jnp.zeros((N_B_LAYERS, BATCH, N_HEADS, HEAD_DIM, HEAD_DIM), state_wkv.dtype)
    y_prompt, _, shift_p, wkv_p = _trunk(x_prompt, shift_zero, wkv_zero, p)
    y_sample, chunk_v_s, shift_s, wkv_s = _trunk(x_sample, state_shift, state_wkv, p)
    chunk_v_sample = jnp.stack(chunk_v_s)
    shift_prompt = jnp.stack(shift_p)
    wkv_prompt = jnp.stack(wkv_p)
    shift_sample = jnp.stack(shift_s)
    wkv_sample = jnp.stack(wkv_s)
    return (y_prompt, y_sample, chunk_v_sample, shift_prompt, wkv_prompt, shift_sample, wkv_sample)
```

```python
import functools

import jax
import jax.numpy as jnp
from jax import lax
from jax.experimental import pallas as pl
from jax.experimental.pallas import tpu as pltpu

F32 = jnp.float32
BF16 = jnp.bfloat16

RMS_EPS = 1e-6
LN_EPS = 1e-5
GN_EPS = 64e-5
HEAD_DIM = 64
CHUNK = 128
A_GROUPS = 8
N_EXPERTS = 8
LANES = 128
MXU_DIM = 256
VMEM_LIMIT = 56 * 1024 * 1024

TM_NORM = 512
TM_SGU = 256
TM_FFN = 512
TM_RWKV = 256
TM_ROUTE = 512
SCAN_SEQS = 2
SCAN_STEPS = 256


def _cparams(n_axes):
    return pltpu.CompilerParams(dimension_semantics=("arbitrary",) * n_axes,
                                vmem_limit_bytes=VMEM_LIMIT)


def _dot(a, b):
    return jnp.dot(a, b, preferred_element_type=F32)


def _rms(x, g):
    ms = jnp.mean(x * x, axis=-1, keepdims=True)
    return x * lax.rsqrt(ms + RMS_EPS) * g


def _gelu(z):
    return 0.5 * z * (1.0 + lax.erf(z * 0.7071067811865476))


def _full(shape):
    n = len(shape)
    return pl.BlockSpec(shape, lambda *_: (0,) * n)


def _rows(tm, d):
    return pl.BlockSpec((tm, d), lambda i, *_: (i, 0))


def _norm_kernel(x_ref, g_ref, h_ref):
    h_ref[...] = _rms(x_ref[...], g_ref[...])


def _norm(x, g):
    m, d = x.shape
    return pl.pallas_call(
        _norm_kernel, grid=(m // TM_NORM,),
        in_specs=[_rows(TM_NORM, d), _full((1, d))],
        out_specs=_rows(TM_NORM, d),
        out_shape=jax.ShapeDtypeStruct((m, d), F32),
        compiler_params=_cparams(1), name="rmsnorm",
    )(x, g)


def _sgu_kernel(x_ref, h_ref, wu_ref, wv_ref, bu_ref, bv_ref, lng_ref, lnb_ref, wc_ref, bs_ref,
                wout_ref, gn_ref, xn_ref, hn_ref, vn_ref, gated_scr):
    tm = x_ref.shape[0]
    aw = vn_ref.shape[1]
    gd = aw // A_GROUPS
    hb = h_ref[...].astype(BF16)
    zv = _gelu(_dot(hb, wv_ref[...]) + bv_ref[...])
    mu = jnp.mean(zv, axis=-1, keepdims=True)
    zc = zv - mu
    var = jnp.mean(zc * zc, axis=-1, keepdims=True)
    vn_ref[...] = zc * lax.rsqrt(var + LN_EPS) * lng_ref[...] + lnb_ref[...]
    for g in range(A_GROUPS):
        cs = slice(g * gd, (g + 1) * gd)
        zu = _gelu(_dot(hb, wu_ref[:, cs]) + bu_ref[:, cs])
        for c in range(tm // CHUNK):
            rs = slice(c * CHUNK, (c + 1) * CHUNK)
            mixed = _dot(wc_ref[0, g], vn_ref[rs, cs].astype(BF16)) + bs_ref[0, :, cs]
            gated_scr[rs, cs] = (zu[rs] * mixed).astype(BF16)
    xn = x_ref[...] + _dot(gated_scr[...], wout_ref[...])
    xn_ref[...] = xn
    hn_ref[...] = _rms(xn, gn_ref[...])


def _sgu(x, h, wu, wv, bu, bv, lng, lnb, wc2, bs2, wout, gn, n_prompt_rows):
    m, d = x.shape
    aw = wu.shape[1]
    tm = TM_SGU
    n_prompt_tiles = n_prompt_rows // tm
    sel4 = lambda i: (jnp.where(i >= n_prompt_tiles, 1, 0), 0, 0, 0)
    sel3 = lambda i: (jnp.where(i >= n_prompt_tiles, 1, 0), 0, 0)
    return pl.pallas_call(
        _sgu_kernel, grid=(m // tm,),
        in_specs=[_rows(tm, d), _rows(tm, d), _full((d, aw)), _full((d, aw)), _full((1, aw)), _full((1, aw)),
                  _full((1, aw)), _full((1, aw)),
                  pl.BlockSpec((1, A_GROUPS, CHUNK, CHUNK), sel4),
                  pl.BlockSpec((1, CHUNK, aw), sel3),
                  _full((aw, d)), _full((1, d))],
        out_specs=[_rows(tm, d), _rows(tm, d), _rows(tm, aw)],
        out_shape=[jax.ShapeDtypeStruct((m, d), F32), jax.ShapeDtypeStruct((m, d), F32),
                   jax.ShapeDtypeStruct((m, aw), F32)],
        scratch_shapes=[pltpu.VMEM((tm, aw), BF16)],
        compiler_params=_cparams(1), name="sgu",
    )(x, h, wu, wv, bu, bv, lng, lnb, wc2, bs2, wout, gn)


def _swiglu_part(hb, wg_ref, wu_ref, wd_ref):
    a = _dot(hb, wg_ref[...])
    b = _dot(hb, wu_ref[...])
    s = (a * jax.nn.sigmoid(a) * b).astype(BF16)
    return _dot(s, wd_ref[...])


def _ffn_kernel(x_ref, h_ref, wg_ref, wu_ref, wd_ref, gn_ref, xn_ref, hn_ref, hb_scr, acc_scr):
    k = pl.program_id(1)

    @pl.when(k == 0)
    def _():
        hb_scr[...] = h_ref[...].astype(BF16)

    part = _swiglu_part(hb_scr[...], wg_ref, wu_ref, wd_ref)

    @pl.when(k == 0)
    def _():
        acc_scr[...] = part

    @pl.when(k > 0)
    def _():
        acc_scr[...] += part

    @pl.when(k == pl.num_programs(1) - 1)
    def _():
        xn = x_ref[...] + acc_scr[...]
        xn_ref[...] = xn
        hn_ref[...] = _rms(xn, gn_ref[...])


def _ffn_tiles(f):
    nk = 2 if f % (2 * LANES) == 0 else 1
    return nk, f // nk


def _ffn(x, h, wg, wu, wd, gn):
    m, d = x.shape
    f = wg.shape[1]
    nk, tf = _ffn_tiles(f)
    tm = TM_FFN
    return pl.pallas_call(
        _ffn_kernel, grid=(m // tm, nk),
        in_specs=[_rows(tm, d), _rows(tm, d),
                  pl.BlockSpec((d, tf), lambda i, k: (0, k)), pl.BlockSpec((d, tf), lambda i, k: (0, k)),
                  pl.BlockSpec((tf, d), lambda i, k: (k, 0)), _full((1, d))],
        out_specs=[_rows(tm, d), _rows(tm, d)],
        out_shape=[jax.ShapeDtypeStruct((m, d), F32), jax.ShapeDtypeStruct((m, d), F32)],
        scratch_shapes=[pltpu.VMEM((tm, d), BF16), pltpu.VMEM((tm, d), F32)],
        compiler_params=_cparams(2), name="swiglu",
    )(x, h, wg, wu, wd, gn)


def _moe_ffn_kernel(te_ref, xs_ref, wg_ref, wu_ref, wd_ref, ys_ref, hb_scr, acc_scr):
    i = pl.program_id(0)
    k = pl.program_id(1)
    nk = pl.num_programs(1)
    n_used = te_ref[pl.num_programs(0)]

    @pl.when(i < n_used)
    def _():
        @pl.when(k == 0)
        def _():
            hb_scr[...] = xs_ref[...].astype(BF16)

        part = _swiglu_part(hb_scr[...], wg_ref, wu_ref, wd_ref)

        @pl.when(k == 0)
        def _():
            acc_scr[...] = part

        @pl.when(k > 0)
        def _():
            acc_scr[...] += part

        @pl.when(k == nk - 1)
        def _():
            ys_ref[...] = acc_scr[...]

    @pl.when((i >= n_used) & (k == nk - 1))
    def _():
        ys_ref[...] = jnp.zeros_like(ys_ref)


def _moe_ffn(te, xs, wg, wu, wd):
    s, d = xs.shape
    f = wg.shape[2]
    nk, tf = _ffn_tiles(f)
    tg = TM_FFN
    grid_spec = pltpu.PrefetchScalarGridSpec(
        num_scalar_prefetch=1, grid=(s // tg, nk),
        in_specs=[pl.BlockSpec((tg, d), lambda i, k, te: (i, 0)),
                  pl.BlockSpec((None, d, tf), lambda i, k, te: (te[i], 0, k)),
                  pl.BlockSpec((None, d, tf), lambda i, k, te: (te[i], 0, k)),
                  pl.BlockSpec((None, tf, d), lambda i, k, te: (te[i], k, 0))],
        out_specs=pl.BlockSpec((tg, d), lambda i, k, te: (i, 0)),
        scratch_shapes=[pltpu.VMEM((tg, d), BF16), pltpu.VMEM((tg, d), F32)])
    return pl.pallas_call(
        _moe_ffn_kernel, grid_spec=grid_spec,
        out_shape=jax.ShapeDtypeStruct((s, d), F32),
        compiler_params=_cparams(2), name="moe_swiglu",
    )(te, xs, wg, wu, wd)


def _head_sum(x, e_ref, et_ref):
    s = _dot(x.astype(BF16), e_ref[...])
    s1 = s.astype(BF16)
    r1 = s - s1.astype(F32)
    s2 = r1.astype(BF16)
    s3 = (r1 - s2.astype(F32)).astype(BF16)
    et = et_ref[...]
    return _dot(s1, et) + _dot(s2, et) + _dot(s3, et)


def _softplus(z):
    return jnp.maximum(z, 0.0) + jnp.log1p(jnp.exp(-jnp.abs(z)))


def _rwkv_proj_kernel(*refs, vres):
    if vres:
        (h_ref, hp_ref, mu_ref, wrkv_ref, w0_ref, w1_ref, w2_ref, a0_ref, a1_ref, a2_ref, g1_ref, g2_ref,
         kk_ref_, ka_ref, e_ref, et_ref, v0_ref, v1_ref, v2_ref, vf_ref,
         r_o, w_o, k_o, v_o, kk_o, kka_o, g_o) = refs
    else:
        (h_ref, hp_ref, mu_ref, wrkv_ref, w0_ref, w1_ref, w2_ref, a0_ref, a1_ref, a2_ref, g1_ref, g2_ref,
         kk_ref_, ka_ref, e_ref, et_ref,
         r_o, w_o, k_o, v_o, kk_o, kka_o, g_o) = refs
    h = h_ref[...]
    xx = hp_ref[...] - h

    def mix(s):
        return (h + xx * mu_ref[s:s + 1, :]).astype(BF16)

    r_o[...] = _dot(mix(0), wrkv_ref[0])
    k = _dot(mix(1), wrkv_ref[1])
    xv = mix(2)
    v = _dot(xv, wrkv_ref[2])
    wl = w0_ref[...] + _dot(jnp.tanh(_dot(mix(3), w1_ref[...])).astype(BF16), w2_ref[...])
    w_log = -_softplus(-wl) - 0.5
    w_o[...] = jnp.exp(-jnp.exp(w_log))
    alpha = jax.nn.sigmoid(a0_ref[...] + _dot(_dot(mix(4), a1_ref[...]).astype(BF16), a2_ref[...]))
    if vres:
        gate = jax.nn.sigmoid(v0_ref[...] + _dot(_dot(xv, v1_ref[...]).astype(BF16), v2_ref[...]))
        v = v + (vf_ref[...] - v) * gate
    v_o[...] = v
    g_o[...] = _dot(jax.nn.sigmoid(_dot(mix(5), g1_ref[...])).astype(BF16), g2_ref[...])
    kk = k * kk_ref_[...]
    nrm = jnp.maximum(jnp.sqrt(_head_sum(kk * kk, e_ref, et_ref)), 1e-12)
    kk = kk / nrm
    kk_o[...] = kk
    kka_o[...] = kk * alpha
    k_o[...] = k * (1.0 + (alpha - 1.0) * ka_ref[...])


def _rwkv_proj(h, hp, mu, wrkv, w0, w1, w2, a0, a1, a2, g1, g2, k_k, k_a, e, et, vres_args):
    m, d = h.shape
    tm = TM_RWKV
    vres = vres_args is not None
    args = [h, hp, mu, wrkv, w0, w1, w2, a0, a1, a2, g1, g2, k_k, k_a, e, et]
    specs = [_rows(tm, d), _rows(tm, d)] + [_full(a.shape) for a in args[2:]]
    if vres:
        v0, v1, v2, vf = vres_args
        args += [v0, v1, v2, vf]
        specs += [_full(v0.shape), _full(v1.shape), _full(v2.shape), _rows(tm, d)]
    return pl.pallas_call(
        functools.partial(_rwkv_proj_kernel, vres=vres), grid=(m // tm,),
        in_specs=specs,
        out_specs=[_rows(tm, d)] * 7,
        out_shape=[jax.ShapeDtypeStruct((m, d), F32)] * 7,
        compiler_params=_cparams(1), name="rwkv_proj",
    )(*args)


def _scan_kernel(*refs, nb, tb, nq, carry):
    if carry:
        kk_ref, w_ref, kka_ref, k_ref, r_ref, v_ref, y_ref, st_ref = refs[:8]
        st_scrs = refs[8:]
    else:
        kk_ref, w_ref, kka_ref, k_ref, r_ref, v_ref, s0_ref, y_ref, st_ref = refs[:9]
        st_scrs = refs[9:]
    width = 4 * HEAD_DIM
    if carry:
        @pl.when(pl.program_id(1) == 0)
        def _():
            for s_scr in st_scrs:
                s_scr[...] = jnp.zeros_like(s_scr)
    else:
        for s in range(nb):
            for q in range(nq):
                st_scrs[s * nq + q][...] = s0_ref[s, q]
    sub = lax.broadcasted_iota(jnp.int32, (HEAD_DIM, width), 0)
    lane = lax.broadcasted_iota(jnp.int32, (HEAD_DIM, width), 1)
    dmask = sub == (lane % HEAD_DIM)
    rr = lax.broadcasted_iota(jnp.int32, (width, width), 0)
    cc = lax.broadcasted_iota(jnp.int32, (width, width), 1)
    ones_blk = ((rr // HEAD_DIM) == (cc // HEAD_DIM)).astype(BF16)
    n_lhs = 3 * HEAD_DIM

    def rows(ref, s, t):
        x8 = ref[s, t]
        return [jnp.concatenate([x8[2 * q:2 * q + 1, :], x8[2 * q + 1:2 * q + 2, :]], axis=1) for q in range(nq)]

    def y_rows(yb):
        ys = []
        for q in range(nq):
            y = jnp.sum(jnp.where(dmask, yb[q * HEAD_DIM:(q + 1) * HEAD_DIM], 0.0), axis=0, keepdims=True)
            ys += [y[:, :LANES], y[:, LANES:]]
        return jnp.concatenate(ys, axis=0)

    def step(t, q_prev):
        tprev = jnp.maximum(t - 1, 0)
        q_next = []
        for s in range(nb):
            kk = rows(kk_ref, s, t)
            w = rows(w_ref, s, t)
            kka = rows(kka_ref, s, t)
            k = rows(k_ref, s, t)
            r = rows(r_ref, s, t)
            v = rows(v_ref, s, t)
            st = [st_scrs[s * nq + q][...] for q in range(nq)]
            lhs = []
            for q in range(nq):
                p = st[q] * kk[q]
                ph = p.astype(BF16)
                plo = (p - ph.astype(F32)).astype(BF16)
                vd = jnp.where(dmask, v[q], 0.0).astype(BF16)
                lhs += [ph, plo, vd]
            lhs.append(q_prev[s])
            o = _dot(jnp.concatenate(lhs, axis=0), ones_blk)
            qs = []
            for q in range(nq):
                base = q * n_lhs
                skk = o[base:base + HEAD_DIM] + o[base + HEAD_DIM:base + 2 * HEAD_DIM]
                vb = o[base + 2 * HEAD_DIM:base + 3 * HEAD_DIM]
                sn = st[q] * w[q] - skk * kka[q] + vb * k[q]
                st_scrs[s * nq + q][...] = sn
                qs.append((sn * r[q]).astype(BF16))
            y_ref[s, tprev] = y_rows(o[nq * n_lhs:])
            q_next.append(jnp.concatenate(qs, axis=0))
        return tuple(q_next)

    q_zero = jnp.zeros((nq * HEAD_DIM, width), BF16)
    q_last = lax.fori_loop(0, tb, step, (q_zero,) * nb)
    for s in range(nb):
        y_ref[s, tb - 1] = y_rows(_dot(q_last[s], ones_blk))

    def write_state():
        for s in range(nb):
            for q in range(nq):
                st_ref[s, q] = st_scrs[s * nq + q][...]

    if carry:
        pl.when(pl.program_id(1) == pl.num_programs(1) - 1)(write_state)
    else:
        write_state()


def _scan(ops, s0, n_seq, t_len, seq0):
    nb = SCAN_SEQS
    nq = ops[0].shape[2] * LANES // (4 * HEAD_DIM)
    carry = s0 is None
    tb = min(t_len, SCAN_STEPS)
    sb0 = seq0 // nb
    row_spec = pl.BlockSpec((nb, tb, ops[0].shape[2], LANES), lambda b, c: (sb0 + b, c, 0, 0))
    out_row = pl.BlockSpec((nb, tb, ops[0].shape[2], LANES), lambda b, c: (b, c, 0, 0))
    st_spec = pl.BlockSpec((nb, nq, HEAD_DIM, 4 * HEAD_DIM), lambda b, c: (b, 0, 0, 0))
    in_specs = [row_spec] * 6 + ([] if carry else [st_spec])
    args = list(ops) + ([] if carry else [s0])
    return pl.pallas_call(
        functools.partial(_scan_kernel, nb=nb, tb=tb, nq=nq, carry=carry),
        grid=(n_seq // nb, t_len // tb),
        in_specs=in_specs,
        out_specs=[out_row, st_spec],
        out_shape=[jax.ShapeDtypeStruct((n_seq, t_len) + ops[0].shape[2:], F32),
                   jax.ShapeDtypeStruct((n_seq, nq, HEAD_DIM, 4 * HEAD_DIM), F32)],
        scratch_shapes=[pltpu.VMEM((HEAD_DIM, 4 * HEAD_DIM), F32)] * (nb * nq),
        compiler_params=_cparams(2), name="wkv_scan",
    )(*args)


def _rwkv_out_kernel(x_ref, y_ref, r_ref, k_ref, v_ref, g_ref, rk_ref, lg_ref, lb_ref, e_ref, et_ref,
                     wout_ref, gn_ref, xn_ref, hn_ref):
    inv_n = 1.0 / HEAD_DIM
    y = y_ref[...]
    yc = y - _head_sum(y, e_ref, et_ref) * inv_n
    var = _head_sum(yc * yc, e_ref, et_ref) * inv_n
    yn = yc * lax.rsqrt(var + GN_EPS) * lg_ref[...] + lb_ref[...]
    bonus = _head_sum(r_ref[...] * k_ref[...] * rk_ref[...], e_ref, et_ref) * v_ref[...]
    o = _dot(((yn + bonus) * g_ref[...]).astype(BF16), wout_ref[...])
    xn = x_ref[...] + o
    xn_ref[...] = xn
    hn_ref[...] = _rms(xn, gn_ref[...])


def _rwkv_out(x, y, r, k, v, g, rk, lg, lb, e, et, wout, gn):
    m, d = x.shape
    tm = TM_RWKV
    small = [rk, lg, lb, e, et, wout, gn]
    return pl.pallas_call(
        _rwkv_out_kernel, grid=(m // tm,),
        in_specs=[_rows(tm, d)] * 6 + [_full(a.shape) for a in small],
        out_specs=[_rows(tm, d)] * 2,
        out_shape=[jax.ShapeDtypeStruct((m, d), F32)] * 2,
        compiler_params=_cparams(1), name="rwkv_out",
    )(x, y, r, k, v, g, *small)


def _router_kernel(h_ref, wr_ref, idx_ref, wts_ref, cnt_ref, run_scr):
    tm = h_ref.shape[0]

    @pl.when(pl.program_id(0) == 0)
    def _():
        run_scr[...] = jnp.zeros_like(run_scr)

    lane = lax.broadcasted_iota(jnp.int32, (tm, LANES), 1)
    lanef = lane.astype(F32)
    logits = jnp.dot(h_ref[...], wr_ref[...], preferred_element_type=F32, precision=lax.Precision.HIGHEST)
    neg = jnp.float32(-jnp.inf)
    logits = jnp.where(lane < N_EXPERTS, logits, neg)
    m1 = jnp.max(logits, axis=-1, keepdims=True)
    i1 = jnp.min(jnp.where(logits == m1, lanef, float(LANES)), axis=-1, keepdims=True)
    l2 = jnp.where(lanef == i1, neg, logits)
    m2 = jnp.max(l2, axis=-1, keepdims=True)
    i2 = jnp.min(jnp.where(l2 == m2, lanef, float(LANES)), axis=-1, keepdims=True)
    ex = jnp.exp(m2 - m1)
    den = 1.0 + ex
    w1 = 1.0 / den
    w2 = ex / den
    oh1 = lanef == i1
    oh2 = lanef == i2
    onehot = jnp.where(oh1 | oh2, 1.0, 0.0)
    ri = lax.broadcasted_iota(jnp.int32, (tm, tm), 0)
    ci = lax.broadcasted_iota(jnp.int32, (tm, tm), 1)
    lower = (ci < ri).astype(BF16)
    cum = _dot(lower, onehot.astype(BF16)) + run_scr[0:1, :]
    rank1 = jnp.sum(jnp.where(oh1, cum, 0.0), axis=-1, keepdims=True)
    rank2 = jnp.sum(jnp.where(oh2, cum, 0.0), axis=-1, keepdims=True)
    run = run_scr[0:1, :] + jnp.sum(onehot, axis=0, keepdims=True)
    run_scr[...] = jnp.broadcast_to(run, run_scr.shape)
    cnt_ref[0] = jnp.broadcast_to(run, cnt_ref.shape[1:]).astype(jnp.int32)
    packed = jnp.where(lane == 0, i1, jnp.where(lane == 1, i2, jnp.where(lane == 2, rank1, jnp.where(lane == 3, rank2, 0.0))))
    idx_ref[...] = packed.astype(jnp.int32)
    wts_ref[...] = jnp.where(lane == 0, w1, jnp.where(lane == 1, w2, 0.0))


def _router(h, wr):
    m, d = h.shape
    tm = TM_ROUTE
    nt = m // tm
    return pl.pallas_call(
        _router_kernel, grid=(nt,),
        in_specs=[_rows(tm, d), _full((d, LANES))],
        out_specs=[_rows(tm, LANES), _rows(tm, LANES), pl.BlockSpec((1, 8, LANES), lambda i: (i, 0, 0))],
        out_shape=[jax.ShapeDtypeStruct((m, LANES), jnp.int32), jax.ShapeDtypeStruct((m, LANES), F32),
                   jax.ShapeDtypeStruct((nt, 8, LANES), jnp.int32)],
        scratch_shapes=[pltpu.VMEM((8, LANES), F32)],
        compiler_params=_cparams(1), name="router",
    )(h, wr)


def _row_copy(src, src_row, dst, dst_row, sem):
    return pltpu.make_async_copy(src.at[pl.ds(src_row, 1)], dst.at[pl.ds(dst_row, 1)], sem)


def _dispatch_kernel(slots_ref, h_hbm, xs_in, xs_hbm, sem):
    del xs_in
    tm = slots_ref.shape[2] // 2
    base = pl.program_id(0) * tm

    def issue(r, c):
        _row_copy(h_hbm, base + r, xs_hbm, slots_ref[0, 0, r], sem).start()
        _row_copy(h_hbm, base + r, xs_hbm, slots_ref[0, 0, tm + r], sem).start()
        return c

    lax.fori_loop(0, tm, issue, 0)

    def drain(r, c):
        _row_copy(h_hbm, 0, xs_hbm, 0, sem).wait()
        _row_copy(h_hbm, 0, xs_hbm, 0, sem).wait()
        return c

    lax.fori_loop(0, tm, drain, 0)


def _dispatch(slots, h, n_slots):
    m, d = h.shape
    nt = slots.shape[0]
    xs0 = jnp.zeros((n_slots, d), F32)
    return pl.pallas_call(
        _dispatch_kernel, grid=(nt,),
        in_specs=[pl.BlockSpec((1, 1, slots.shape[2]), lambda i: (i, 0, 0), memory_space=pltpu.SMEM),
                  pl.BlockSpec(memory_space=pl.ANY), pl.BlockSpec(memory_space=pl.ANY)],
        out_specs=pl.BlockSpec(memory_space=pl.ANY),
        out_shape=jax.ShapeDtypeStruct((n_slots, d), F32),
        scratch_shapes=[pltpu.SemaphoreType.DMA(())],
        input_output_aliases={2: 0},
        compiler_params=_cparams(1), name="moe_dispatch",
    )(slots, h, xs0)


def _combine_kernel(slots_ref, x_ref, wts_ref, ys_hbm, gn_ref, xn_ref, hn_ref, buf, sem):
    tm = x_ref.shape[0]

    def issue(r, c):
        _row_copy(ys_hbm, slots_ref[0, 0, r], buf.at[0], r, sem).start()
        _row_copy(ys_hbm, slots_ref[0, 0, tm + r], buf.at[1], r, sem).start()
        return c

    lax.fori_loop(0, tm, issue, 0)

    def drain(r, c):
        _row_copy(ys_hbm, 0, buf.at[0], 0, sem).wait()
        _row_copy(ys_hbm, 0, buf.at[1], 0, sem).wait()
        return c

    lax.fori_loop(0, tm, drain, 0)
    wts = wts_ref[...]
    xn = x_ref[...] + wts[:, 0:1] * buf[0] + wts[:, 1:2] * buf[1]
    xn_ref[...] = xn
    hn_ref[...] = _rms(xn, gn_ref[...])


def _combine(slots, x, wts, ys, gn):
    m, d = x.shape
    tm = TM_ROUTE
    return pl.pallas_call(
        _combine_kernel, grid=(m // tm,),
        in_specs=[pl.BlockSpec((1, 1, slots.shape[2]), lambda i: (i, 0, 0), memory_space=pltpu.SMEM),
                  _rows(tm, d), _rows(tm, LANES), pl.BlockSpec(memory_space=pl.ANY), _full((1, d))],
        out_specs=[_rows(tm, d)] * 2,
        out_shape=[jax.ShapeDtypeStruct((m, d), F32)] * 2,
        scratch_shapes=[pltpu.VMEM((2, tm, d), F32), pltpu.SemaphoreType.DMA(())],
        compiler_params=_cparams(1), name="moe_combine",
    )(slots, x, wts, ys, gn)


def _moe(x, h, router, wg, wu, wd, gn):
    m, d = x.shape
    tg = TM_FFN
    tm = TM_ROUTE
    wr = jnp.zeros((d, LANES), F32).at[:, :N_EXPERTS].set(router)
    idx, wts, cnt = _router(h, wr)
    counts = cnt[-1, 0, :N_EXPERTS]
    padded = ((counts + tg - 1) // tg) * tg
    ends = jnp.cumsum(padded)
    offs = ends - padded
    n_tiles = (2 * m) // tg + N_EXPERTS
    slot1 = offs[idx[:, 0]] + idx[:, 2]
    slot2 = offs[idx[:, 1]] + idx[:, 3]
    slots = jnp.concatenate([slot1.reshape(m // tm, 1, tm), slot2.reshape(m // tm, 1, tm)], axis=2).astype(jnp.int32)
    starts = jnp.arange(n_tiles, dtype=jnp.int32) * tg
    tile_expert = jnp.minimum(jnp.sum(ends[None, :] <= starts[:, None], axis=1), N_EXPERTS - 1)
    te = jnp.concatenate([tile_expert, ends[-1:] // tg]).astype(jnp.int32)
    xs = _dispatch(slots, h, n_tiles * tg)
    ys = _moe_ffn(te, xs, wg, wu, wd)
    return _combine(slots, x, wts, ys, gn)


def _pad_lora(a, b):
    rnk = a.shape[1]
    rp = -(-rnk // LANES) * LANES
    a = jnp.pad(a, ((0, 0), (0, rp - rnk)))
    b = jnp.pad(b, ((0, rp - rnk), (0, 0)))
    return a.astype(BF16), b.astype(BF16)


def kernel(x_prompt, x_sample, state_shift, state_wkv, norm_mix_g, norm_ffn_g, norm_out_g, a_w_in, a_b_in, a_ln_g, a_ln_b, a_w_s, a_b_s, a_w_out, b_mu, b_w_rkv, b_w0, b_w1, b_w2, b_a0, b_a1, b_a2, b_v0, b_v1, b_v2, b_g1, b_g2, b_k_k, b_k_a, b_r_k, b_lnx_g, b_lnx_b, b_w_out, f_w_gate, f_w_up, f_w_down, m_router, m_w_gate, m_w_up, m_w_down):
    bp, tp, d = x_prompt.shape
    bs, ts, _ = x_sample.shape
    depth = norm_mix_g.shape[0]
    n_heads = d // HEAD_DIM
    mp = bp * tp
    ms = bs * ts
    aw = a_w_out.shape[1]
    gd = aw // A_GROUPS
    row = lambda a: a.reshape(1, -1)

    x = jnp.concatenate([x_prompt.reshape(mp, d), x_sample.reshape(ms, d)], axis=0)
    gains = [(norm_mix_g[i], norm_ffn_g[i]) for i in range(depth)]
    next_gain = lambda i, sub: row(gains[i][1]) if sub == 0 else row(norm_mix_g[i + 1] if i + 1 < depth else norm_out_g)

    head_of = jnp.arange(d) // HEAD_DIM
    e_mat = (head_of[:, None] == jnp.arange(LANES)[None, :]).astype(BF16)
    et_mat = e_mat.T

    h = _norm(x, row(norm_mix_g[0]))
    chunk_v, shift_p, wkv_p, shift_s, wkv_s = [], [], [], [], []
    v_first = None
    for i in range(depth):
        j = i // 2
        if i % 2 == 0:
            w_in = a_w_in[j].astype(BF16)
            tri = jnp.tril(jnp.ones((CHUNK, CHUNK), bool))
            wc_p = jnp.where(tri, a_w_s[j], 0.0)
            rr = jnp.arange(CHUNK)
            same_seq = (rr[:, None] // ts) == (rr[None, :] // ts)
            wc_s = jnp.where(same_seq & tri, a_w_s[j][:, rr[:, None] % ts, rr[None, :] % ts], 0.0)
            wc2 = jnp.stack([wc_p, wc_s]).astype(BF16)
            bs_p = jnp.repeat(a_b_s[j].T, gd, axis=1)
            bs_s = jnp.repeat(a_b_s[j][:, rr % ts].T, gd, axis=1)
            bs2 = jnp.stack([bs_p, bs_s])
            x, h, vn = _sgu(x, h, w_in[:, :aw], w_in[:, aw:], row(a_b_in[j][:aw]), row(a_b_in[j][aw:]),
                            row(a_ln_g[j]), row(a_ln_b[j]), wc2, bs2, a_w_out[j].astype(BF16),
                            next_gain(i, 0), mp)
            chunk_v.append(vn[mp:].reshape(bs, ts, aw))
            x, h = _ffn(x, h, f_w_gate[j].astype(BF16), f_w_up[j].astype(BF16), f_w_down[j].astype(BF16),
                        next_gain(i, 1))
        else:
            hp3 = h[:mp].reshape(bp, tp, d)
            hs3 = h[mp:].reshape(bs, ts, d)
            shift_p.append(hp3[:, -1])
            shift_s.append(hs3[:, -1])
            hprev = jnp.concatenate([
                jnp.concatenate([jnp.zeros((bp, 1, d), F32), hp3[:, :-1]], axis=1).reshape(mp, d),
                jnp.concatenate([state_shift[j][:, None, :], hs3[:, :-1]], axis=1).reshape(ms, d)], axis=0)
            w1, w2 = _pad_lora(b_w1[j], b_w2[j])
            a1, a2 = _pad_lora(b_a1[j], b_a2[j])
            g1, g2 = _pad_lora(b_g1[j], b_g2[j])
            vres_args = None
            if j > 0:
                v1, v2 = _pad_lora(b_v1[j - 1], b_v2[j - 1])
                vres_args = (row(b_v0[j - 1]), v1, v2, v_first)
            r, w, k, v, kk, kka, g = _rwkv_proj(
                h, hprev, b_mu[j], b_w_rkv[j].astype(BF16), row(b_w0[j]), w1, w2, row(b_a0[j]), a1, a2, g1, g2,
                row(b_k_k[j]), row(b_k_a[j]), e_mat, et_mat, vres_args)
            if j == 0:
                v_first = v
            npair = d // LANES
            ops = [kk, w, kka, k, r, v]
            ops_p = [a[:mp].reshape(bp, tp, npair, LANES) for a in ops]
            ops_s = [a[mp:].reshape(bs, ts, npair, LANES) for a in ops]
            nq = n_heads // 4
            s0 = state_wkv[j].reshape(bs, nq, 4, HEAD_DIM, HEAD_DIM).transpose(0, 1, 3, 2, 4).reshape(bs, nq, HEAD_DIM, 4 * HEAD_DIM)
            y_p, st_p = _scan(ops_p, None, bp, tp, 0)
            y_s, st_s = _scan(ops_s, s0, bs, ts, 0)
            untile = lambda s, nbat: s.reshape(nbat, nq, HEAD_DIM, 4, HEAD_DIM).transpose(0, 1, 3, 2, 4).reshape(nbat, n_heads, HEAD_DIM, HEAD_DIM)
            wkv_p.append(untile(st_p, bp))
            wkv_s.append(untile(st_s, bs))
            y = jnp.concatenate([y_p.reshape(mp, d), y_s.reshape(ms, d)], axis=0)
            x, h = _rwkv_out(x, y, r, k, v, g, row(b_r_k[j].reshape(-1)), row(b_lnx_g[j]), row(b_lnx_b[j]),
                             e_mat, et_mat, b_w_out[j].astype(BF16), next_gain(i, 0))
            x, h = _moe(x, h, m_router[j], m_w_gate[j].astype(BF16), m_w_up[j].astype(BF16),
                        m_w_down[j].astype(BF16), next_gain(i, 1))
    y_prompt = h[:mp].reshape(bp, tp, d)
    y_sample = h[mp:].reshape(bs, ts, d)
    return (y_prompt, y_sample, jnp.stack(chunk_v), jnp.stack(shift_p), jnp.stack(wkv_p),
            jnp.stack(shift_s), jnp.stack(wkv_s))
```

```python
import functools

import jax
import jax.numpy as jnp
from jax import lax
from jax.experimental import pallas as pl
from jax.experimental.pallas import tpu as pltpu

F32 = jnp.float32
BF16 = jnp.bfloat16

RMS_EPS = 1e-6
LN_EPS = 1e-5
GN_EPS = 64e-5
HEAD_DIM = 64
CHUNK = 128
A_GROUPS = 8
N_EXPERTS = 8
LANES = 128
MXU_DIM = 256
VMEM_LIMIT = 56 * 1024 * 1024

TM_NORM = 512
TM_SGU = 256
TM_FFN = 512
TM_RWKV = 256
TM_ROUTE = 512
SCAN_SEQS = 2
SCAN_STEPS = 256
SCAN_GROUP = 8
SCAN_PIECES = (2, 1, 1)
SCAN_PIECES_PRECISE = (3, 3, 3)


def _cparams(n_axes):
    return pltpu.CompilerParams(dimension_semantics=("arbitrary",) * n_axes,
                                vmem_limit_bytes=VMEM_LIMIT)


def _dot(a, b):
    return jnp.dot(a, b, preferred_element_type=F32)


def _mm(a, w, precise):
    if precise:
        return jnp.dot(a.astype(F32), w, preferred_element_type=F32, precision=lax.Precision.HIGHEST)
    return jnp.dot(a.astype(BF16), w, preferred_element_type=F32)


def _split(x, n):
    pieces = []
    for _ in range(n - 1):
        p = x.astype(BF16)
        pieces.append(p)
        x = x - p.astype(F32)
    pieces.append(x.astype(BF16))
    return pieces


def _wdtype(precise):
    return F32 if precise else BF16


def _add_all(terms):
    return functools.reduce(lambda a, b: a + b, terms)


def _rms(x, g):
    ms = jnp.mean(x * x, axis=-1, keepdims=True)
    return x * lax.rsqrt(ms + RMS_EPS) * g


def _gelu(z):
    return 0.5 * z * (1.0 + lax.erf(z * 0.7071067811865476))


def _full(shape, single=False):
    n = len(shape)
    if single:
        return pl.BlockSpec(shape, lambda *_: (0,) * n, pipeline_mode=pl.Buffered(1))
    return pl.BlockSpec(shape, lambda *_: (0,) * n)


def _rows(tm, d):
    return pl.BlockSpec((tm, d), lambda i, *_: (i, 0))


def _norm_kernel(x_ref, g_ref, h_ref):
    h_ref[...] = _rms(x_ref[...], g_ref[...])


def _norm(x, g):
    m, d = x.shape
    return pl.pallas_call(
        _norm_kernel, grid=(m // TM_NORM,),
        in_specs=[_rows(TM_NORM, d), _full((1, d))],
        out_specs=_rows(TM_NORM, d),
        out_shape=jax.ShapeDtypeStruct((m, d), F32),
        compiler_params=_cparams(1), name="rmsnorm",
    )(x, g)


def _sgu_kernel(x_ref, h_ref, wu_ref, wv_ref, bu_ref, bv_ref, lng_ref, lnb_ref, wc_ref, bs_ref,
                wout_ref, gn_ref, xn_ref, hn_ref, vn_ref, gated_scr, *, precise):
    tm = x_ref.shape[0]
    aw = vn_ref.shape[1]
    gd = aw // A_GROUPS
    hb = h_ref[...].astype(gated_scr.dtype)
    zv = _gelu(_mm(hb, wv_ref[...], precise) + bv_ref[...])
    mu = jnp.mean(zv, axis=-1, keepdims=True)
    zc = zv - mu
    var = jnp.mean(zc * zc, axis=-1, keepdims=True)
    vn_ref[...] = zc * lax.rsqrt(var + LN_EPS) * lng_ref[...] + lnb_ref[...]
    for g in range(A_GROUPS):
        cs = slice(g * gd, (g + 1) * gd)
        zu = _gelu(_mm(hb, wu_ref[:, cs], precise) + bu_ref[:, cs])
        for c in range(tm // CHUNK):
            rs = slice(c * CHUNK, (c + 1) * CHUNK)
            vn = vn_ref[rs, cs].astype(gated_scr.dtype)
            mixed = _mm(wc_ref[0, g], vn, precise) + bs_ref[0, :, cs]
            gated_scr[rs, cs] = (zu[rs] * mixed).astype(gated_scr.dtype)
    xn = x_ref[...] + _mm(gated_scr[...], wout_ref[...], precise)
    xn_ref[...] = xn
    hn_ref[...] = _rms(xn, gn_ref[...])


def _sgu(x, h, wu, wv, bu, bv, lng, lnb, wc2, bs2, wout, gn, n_prompt_rows, precise=False):
    m, d = x.shape
    aw = wu.shape[1]
    tm = TM_SGU
    n_prompt_tiles = n_prompt_rows // tm
    sel4 = lambda i: (jnp.where(i >= n_prompt_tiles, 1, 0), 0, 0, 0)
    sel3 = lambda i: (jnp.where(i >= n_prompt_tiles, 1, 0), 0, 0)
    return pl.pallas_call(
        functools.partial(_sgu_kernel, precise=precise), grid=(m // tm,),
        in_specs=[_rows(tm, d), _rows(tm, d), _full((d, aw), precise), _full((d, aw), precise),
                  _full((1, aw)), _full((1, aw)), _full((1, aw)), _full((1, aw)),
                  pl.BlockSpec((1, A_GROUPS, CHUNK, CHUNK), sel4),
                  pl.BlockSpec((1, CHUNK, aw), sel3),
                  _full((aw, d), precise), _full((1, d))],
        out_specs=[_rows(tm, d), _rows(tm, d), _rows(tm, aw)],
        out_shape=[jax.ShapeDtypeStruct((m, d), F32), jax.ShapeDtypeStruct((m, d), F32),
                   jax.ShapeDtypeStruct((m, aw), F32)],
        scratch_shapes=[pltpu.VMEM((tm, aw), _wdtype(precise))],
        compiler_params=_cparams(1), name="sgu_precise" if precise else "sgu",
    )(x, h, wu, wv, bu, bv, lng, lnb, wc2, bs2, wout, gn)


def _swiglu_part(hb, wg_ref, wu_ref, wd_ref, precise=False):
    a = _mm(hb, wg_ref[...], precise)
    b = _mm(hb, wu_ref[...], precise)
    return _mm(a * jax.nn.sigmoid(a) * b, wd_ref[...], precise)


def _ffn_kernel(x_ref, h_ref, wg_ref, wu_ref, wd_ref, gn_ref, xn_ref, hn_ref, hb_scr, acc_scr, *, precise):
    k = pl.program_id(1)

    @pl.when(k == 0)
    def _():
        hb_scr[...] = h_ref[...].astype(hb_scr.dtype)

    part = _swiglu_part(hb_scr[...], wg_ref, wu_ref, wd_ref, precise)

    @pl.when(k == 0)
    def _():
        acc_scr[...] = part

    @pl.when(k > 0)
    def _():
        acc_scr[...] += part

    @pl.when(k == pl.num_programs(1) - 1)
    def _():
        xn = x_ref[...] + acc_scr[...]
        xn_ref[...] = xn
        hn_ref[...] = _rms(xn, gn_ref[...])


def _ffn_tiles(f, precise=False):
    if precise and f % MXU_DIM == 0:
        return f // MXU_DIM, MXU_DIM
    nk = 2 if f % (2 * LANES) == 0 else 1
    return nk, f // nk


def _ffn(x, h, wg, wu, wd, gn, precise=False):
    m, d = x.shape
    f = wg.shape[1]
    nk, tf = _ffn_tiles(f, precise)
    tm = TM_FFN
    return pl.pallas_call(
        functools.partial(_ffn_kernel, precise=precise), grid=(m // tm, nk),
        in_specs=[_rows(tm, d), _rows(tm, d),
                  pl.BlockSpec((d, tf), lambda i, k: (0, k)), pl.BlockSpec((d, tf), lambda i, k: (0, k)),
                  pl.BlockSpec((tf, d), lambda i, k: (k, 0)), _full((1, d))],
        out_specs=[_rows(tm, d), _rows(tm, d)],
        out_shape=[jax.ShapeDtypeStruct((m, d), F32), jax.ShapeDtypeStruct((m, d), F32)],
        scratch_shapes=[pltpu.VMEM((tm, d), _wdtype(precise)), pltpu.VMEM((tm, d), F32)],
        compiler_params=_cparams(2), name="swiglu_precise" if precise else "swiglu",
    )(x, h, wg, wu, wd, gn)


def _moe_ffn_kernel(te_ref, xs_ref, wg_ref, wu_ref, wd_ref, ys_ref, hb_scr, acc_scr):
    i = pl.program_id(0)
    k = pl.program_id(1)
    nk = pl.num_programs(1)
    n_used = te_ref[pl.num_programs(0)]

    @pl.when(i < n_used)
    def _():
        @pl.when(k == 0)
        def _():
            hb_scr[...] = xs_ref[...].astype(BF16)

        part = _swiglu_part(hb_scr[...], wg_ref, wu_ref, wd_ref)

        @pl.when(k == 0)
        def _():
            acc_scr[...] = part

        @pl.when(k > 0)
        def _():
            acc_scr[...] += part

        @pl.when(k == nk - 1)
        def _():
            ys_ref[...] = acc_scr[...]

    @pl.when((i >= n_used) & (k == nk - 1))
    def _():
        ys_ref[...] = jnp.zeros_like(ys_ref)


def _moe_ffn(te, xs, wg, wu, wd):
    s, d = xs.shape
    f = wg.shape[2]
    nk, tf = _ffn_tiles(f)
    tg = TM_FFN
    grid_spec = pltpu.PrefetchScalarGridSpec(
        num_scalar_prefetch=1, grid=(s // tg, nk),
        in_specs=[pl.BlockSpec((tg, d), lambda i, k, te: (i, 0)),
                  pl.BlockSpec((None, d, tf), lambda i, k, te: (te[i], 0, k)),
                  pl.BlockSpec((None, d, tf), lambda i, k, te: (te[i], 0, k)),
                  pl.BlockSpec((None, tf, d), lambda i, k, te: (te[i], k, 0))],
        out_specs=pl.BlockSpec((tg, d), lambda i, k, te: (i, 0)),
        scratch_shapes=[pltpu.VMEM((tg, d), BF16), pltpu.VMEM((tg, d), F32)])
    return pl.pallas_call(
        _moe_ffn_kernel, grid_spec=grid_spec,
        out_shape=jax.ShapeDtypeStruct((s, d), F32),
        compiler_params=_cparams(2), name="moe_swiglu",
    )(te, xs, wg, wu, wd)


def _head_sum(x, e_ref, et_ref, precise=False):
    e = e_ref[...]
    et = et_ref[...]
    s = _add_all([_dot(p, e) for p in _split(x, 3 if precise else 1)])
    return _add_all([_dot(p, et) for p in _split(s, 3)])


def _softplus(z):
    return jnp.maximum(z, 0.0) + jnp.log1p(jnp.exp(-jnp.abs(z)))


def _rwkv_proj_kernel(*refs, vres, precise):
    if vres:
        (h_ref, hp_ref, mu_ref, wrkv_ref, w0_ref, w1_ref, w2_ref, a0_ref, a1_ref, a2_ref, g1_ref, g2_ref,
         kk_ref_, ka_ref, e_ref, et_ref, v0_ref, v1_ref, v2_ref, vf_ref,
         r_o, w_o, k_o, v_o, kk_o, kka_o, g_o) = refs
    else:
        (h_ref, hp_ref, mu_ref, wrkv_ref, w0_ref, w1_ref, w2_ref, a0_ref, a1_ref, a2_ref, g1_ref, g2_ref,
         kk_ref_, ka_ref, e_ref, et_ref,
         r_o, w_o, k_o, v_o, kk_o, kka_o, g_o) = refs
    h = h_ref[...]
    xx = hp_ref[...] - h

    mm = functools.partial(_mm, precise=precise)

    def mix(s):
        return (h + xx * mu_ref[s:s + 1, :]).astype(_wdtype(precise))

    r_o[...] = mm(mix(0), wrkv_ref[0])
    k = mm(mix(1), wrkv_ref[1])
    xv = mix(2)
    v = mm(xv, wrkv_ref[2])
    wl = w0_ref[...] + mm(jnp.tanh(mm(mix(3), w1_ref[...])), w2_ref[...])
    w_log = -_softplus(-wl) - 0.5
    w_o[...] = jnp.exp(-jnp.exp(w_log))
    alpha = jax.nn.sigmoid(a0_ref[...] + mm(mm(mix(4), a1_ref[...]), a2_ref[...]))
    if vres:
        gate = jax.nn.sigmoid(v0_ref[...] + mm(mm(xv, v1_ref[...]), v2_ref[...]))
        v = v + (vf_ref[...] - v) * gate
    v_o[...] = v
    g_o[...] = mm(jax.nn.sigmoid(mm(mix(5), g1_ref[...])), g2_ref[...])
    kk = k * kk_ref_[...]
    nrm = jnp.maximum(jnp.sqrt(_head_sum(kk * kk, e_ref, et_ref, precise)), 1e-12)
    kk = kk / nrm
    kk_o[...] = kk
    kka_o[...] = kk * alpha
    k_o[...] = k * (1.0 + (alpha - 1.0) * ka_ref[...])


def _rwkv_proj(h, hp, mu, wrkv, w0, w1, w2, a0, a1, a2, g1, g2, k_k, k_a, e, et, vres_args, precise=False):
    m, d = h.shape
    tm = TM_RWKV
    vres = vres_args is not None
    args = [h, hp, mu, wrkv, w0, w1, w2, a0, a1, a2, g1, g2, k_k, k_a, e, et]
    specs = [_rows(tm, d), _rows(tm, d)] + [_full(a.shape, precise) for a in args[2:]]
    if vres:
        v0, v1, v2, vf = vres_args
        args += [v0, v1, v2, vf]
        specs += [_full(v0.shape), _full(v1.shape), _full(v2.shape), _rows(tm, d)]
    return pl.pallas_call(
        functools.partial(_rwkv_proj_kernel, vres=vres, precise=precise), grid=(m // tm,),
        in_specs=specs,
        out_specs=[_rows(tm, d)] * 7,
        out_shape=[jax.ShapeDtypeStruct((m, d), F32)] * 7,
        compiler_params=_cparams(1), name="rwkv_proj_precise" if precise else "rwkv_proj",
    )(*args)


N_SCAN_OPS = 6


def _scan_kernel(*refs, nb, nq, prompt, seq_steps, pieces):
    n_in = N_SCAN_OPS * (nb if prompt else 1)
    op_refs = refs[:n_in]
    if prompt:
        y_ref, st_ref = refs[n_in:n_in + 2]
        st_scrs = refs[n_in + 2:]
    else:
        s0_ref, y_ref, st_ref = refs[n_in:n_in + 3]
        st_scrs = refs[n_in + 3:]
    width = 4 * HEAD_DIM
    if prompt:
        @pl.when(pl.program_id(1) == 0)
        def _():
            for s_scr in st_scrs:
                s_scr[...] = jnp.zeros_like(s_scr)
    else:
        for s in range(nb):
            for q in range(nq):
                st_scrs[s * nq + q][...] = s0_ref[s, q]
    sub = lax.broadcasted_iota(jnp.int32, (HEAD_DIM, width), 0)
    lane = lax.broadcasted_iota(jnp.int32, (HEAD_DIM, width), 1)
    dmask = sub == (lane % HEAD_DIM)
    rr = lax.broadcasted_iota(jnp.int32, (width, width), 0)
    cc = lax.broadcasted_iota(jnp.int32, (width, width), 1)
    ones_blk = ((rr // HEAD_DIM) == (cc // HEAD_DIM)).astype(BF16)
    n_p, n_v, n_q = pieces
    n_lhs = (n_p + n_v) * HEAD_DIM

    def blocks_sum(o, start, n):
        return _add_all([o[start + i * HEAD_DIM:start + (i + 1) * HEAD_DIM] for i in range(n)])

    def y_row(yb):
        rows_q = nq * HEAD_DIM
        yb = _add_all([yb[i * rows_q:(i + 1) * rows_q] for i in range(n_q)])
        return jnp.concatenate(
            [jnp.sum(jnp.where(dmask, yb[q * HEAD_DIM:(q + 1) * HEAD_DIM], 0.0), axis=0, keepdims=True)
             for q in range(nq)], axis=1)

    def run_group(row, n_steps):
        ys = [[None] * n_steps for _ in range(nb)]
        q_prev = [None] * nb
        for u in range(n_steps):
            for s in range(nb):
                st = [st_scrs[s * nq + q][...] for q in range(nq)]
                lhs = []
                for q in range(nq):
                    lhs += _split(st[q] * row(0, s, u, q), n_p)
                    lhs += _split(jnp.where(dmask, row(5, s, u, q), 0.0), n_v)
                if u > 0:
                    lhs.append(q_prev[s])
                o = _dot(jnp.concatenate(lhs, axis=0), ones_blk)
                qs = [[] for _ in range(n_q)]
                for q in range(nq):
                    skk = blocks_sum(o, q * n_lhs, n_p)
                    vb = blocks_sum(o, q * n_lhs + n_p * HEAD_DIM, n_v)
                    sn = st[q] * row(1, s, u, q) - skk * row(2, s, u, q) + vb * row(3, s, u, q)
                    st_scrs[s * nq + q][...] = sn
                    for i, piece in enumerate(_split(sn * row(4, s, u, q), n_q)):
                        qs[i].append(piece)
                if u > 0:
                    ys[s][u - 1] = y_row(o[nq * n_lhs:])
                q_prev[s] = jnp.concatenate([p for grp in qs for p in grp], axis=0)
        for s in range(nb):
            ys[s][n_steps - 1] = y_row(_dot(q_prev[s], ones_blk))
        return ys

    if prompt:
        def group(g, c):
            r0 = pl.multiple_of(g * SCAN_GROUP, SCAN_GROUP)
            x = [[op_refs[o * nb + s][pl.ds(r0, SCAN_GROUP), :] for s in range(nb)] for o in range(N_SCAN_OPS)]
            ys = run_group(lambda o, s, u, q: x[o][s][u:u + 1, q * width:(q + 1) * width], SCAN_GROUP)
            for s in range(nb):
                y_ref[s, pl.ds(r0, SCAN_GROUP), :] = jnp.concatenate(ys[s], axis=0)
            return c

        lax.fori_loop(0, op_refs[0].shape[0] // SCAN_GROUP, group, 0)
    else:
        x = [op_refs[o][...] for o in range(N_SCAN_OPS)]
        ys = run_group(lambda o, s, u, q: x[o][s * seq_steps + u:s * seq_steps + u + 1, q * width:(q + 1) * width],
                       seq_steps)
        y_ref[...] = jnp.concatenate([y for s in range(nb) for y in ys[s]], axis=0)

    def write_state():
        for s in range(nb):
            for q in range(nq):
                st_ref[s, q] = st_scrs[s * nq + q][...]

    if prompt:
        pl.when(pl.program_id(1) == pl.num_programs(1) - 1)(write_state)
    else:
        write_state()


def _scan_prompt(ops, n_seq, t_len):
    d = ops[0].shape[1]
    nb = SCAN_SEQS
    nq = d // (4 * HEAD_DIM)
    tb = min(t_len, SCAN_STEPS)
    n_chunks = t_len // tb
    in_specs, args = [], []
    for a in ops:
        for s in range(nb):
            in_specs.append(pl.BlockSpec((tb, d), lambda b, c, s=s: ((b * nb + s) * n_chunks + c, 0)))
            args.append(a)
    st_spec = pl.BlockSpec((nb, nq, HEAD_DIM, 4 * HEAD_DIM), lambda b, c: (b, 0, 0, 0))
    return pl.pallas_call(
        functools.partial(_scan_kernel, nb=nb, nq=nq, prompt=True, seq_steps=t_len, pieces=SCAN_PIECES),
        grid=(n_seq // nb, n_chunks),
        in_specs=in_specs,
        out_specs=[pl.BlockSpec((nb, tb, d), lambda b, c: (b, c, 0)), st_spec],
        out_shape=[jax.ShapeDtypeStruct((n_seq, t_len, d), F32),
                   jax.ShapeDtypeStruct((n_seq, nq, HEAD_DIM, 4 * HEAD_DIM), F32)],
        scratch_shapes=[pltpu.VMEM((HEAD_DIM, 4 * HEAD_DIM), F32)] * (nb * nq),
        compiler_params=_cparams(2), name="wkv_scan_prompt",
    )(*args)


def _scan_sample(ops, s0, row0, n_seq, t_len, precise=False):
    d = ops[0].shape[1]
    nq = d // (4 * HEAD_DIM)
    nb = SCAN_GROUP // t_len
    rows = nb * t_len
    blk0 = row0 // rows
    row_spec = pl.BlockSpec((rows, d), lambda b: (blk0 + b, 0))
    st_spec = pl.BlockSpec((nb, nq, HEAD_DIM, 4 * HEAD_DIM), lambda b: (b, 0, 0, 0))
    return pl.pallas_call(
        functools.partial(_scan_kernel, nb=nb, nq=nq, prompt=False, seq_steps=t_len,
                          pieces=SCAN_PIECES_PRECISE if precise else SCAN_PIECES),
        grid=(n_seq // nb,),
        in_specs=[row_spec] * N_SCAN_OPS + [st_spec],
        out_specs=[pl.BlockSpec((rows, d), lambda b: (b, 0)), st_spec],
        out_shape=[jax.ShapeDtypeStruct((n_seq * t_len, d), F32),
                   jax.ShapeDtypeStruct((n_seq, nq, HEAD_DIM, 4 * HEAD_DIM), F32)],
        scratch_shapes=[pltpu.VMEM((HEAD_DIM, 4 * HEAD_DIM), F32)] * (nb * nq),
        compiler_params=_cparams(1), name="wkv_scan_sample",
    )(*ops, s0)


def _rwkv_out_kernel(x_ref, y_ref, r_ref, k_ref, v_ref, g_ref, rk_ref, lg_ref, lb_ref, e_ref, et_ref,
                     wout_ref, gn_ref, xn_ref, hn_ref, *, precise):
    inv_n = 1.0 / HEAD_DIM
    hsum = functools.partial(_head_sum, e_ref=e_ref, et_ref=et_ref, precise=precise)
    y = y_ref[...]
    yc = y - hsum(y) * inv_n
    var = hsum(yc * yc) * inv_n
    yn = yc * lax.rsqrt(var + GN_EPS) * lg_ref[...] + lb_ref[...]
    bonus = hsum(r_ref[...] * k_ref[...] * rk_ref[...]) * v_ref[...]
    o = _mm((yn + bonus) * g_ref[...], wout_ref[...], precise)
    xn = x_ref[...] + o
    xn_ref[...] = xn
    hn_ref[...] = _rms(xn, gn_ref[...])


def _rwkv_out(x, y, r, k, v, g, rk, lg, lb, e, et, wout, gn, precise=False):
    m, d = x.shape
    tm = TM_RWKV
    small = [rk, lg, lb, e, et, wout, gn]
    return pl.pallas_call(
        functools.partial(_rwkv_out_kernel, precise=precise), grid=(m // tm,),
        in_specs=[_rows(tm, d)] * 6 + [_full(a.shape) for a in small],
        out_specs=[_rows(tm, d)] * 2,
        out_shape=[jax.ShapeDtypeStruct((m, d), F32)] * 2,
        compiler_params=_cparams(1), name="rwkv_out_precise" if precise else "rwkv_out",
    )(x, y, r, k, v, g, *small)


def _router_kernel(h_ref, wr_ref, idx_ref, wts_ref, cnt_ref, run_scr):
    tm = h_ref.shape[0]

    @pl.when(pl.program_id(0) == 0)
    def _():
        run_scr[...] = jnp.zeros_like(run_scr)

    lane = lax.broadcasted_iota(jnp.int32, (tm, LANES), 1)
    lanef = lane.astype(F32)
    logits = jnp.dot(h_ref[...], wr_ref[...], preferred_element_type=F32, precision=lax.Precision.HIGHEST)
    neg = jnp.float32(-jnp.inf)
    logits = jnp.where(lane < N_EXPERTS, logits, neg)
    m1 = jnp.max(logits, axis=-1, keepdims=True)
    i1 = jnp.min(jnp.where(logits == m1, lanef, float(LANES)), axis=-1, keepdims=True)
    l2 = jnp.where(lanef == i1, neg, logits)
    m2 = jnp.max(l2, axis=-1, keepdims=True)
    i2 = jnp.min(jnp.where(l2 == m2, lanef, float(LANES)), axis=-1, keepdims=True)
    ex = jnp.exp(m2 - m1)
    den = 1.0 + ex
    w1 = 1.0 / den
    w2 = ex / den
    oh1 = lanef == i1
    oh2 = lanef == i2
    onehot = jnp.where(oh1 | oh2, 1.0, 0.0)
    ri = lax.broadcasted_iota(jnp.int32, (tm, tm), 0)
    ci = lax.broadcasted_iota(jnp.int32, (tm, tm), 1)
    lower = (ci < ri).astype(BF16)
    cum = _dot(lower, onehot.astype(BF16)) + run_scr[0:1, :]
    rank1 = jnp.sum(jnp.where(oh1, cum, 0.0), axis=-1, keepdims=True)
    rank2 = jnp.sum(jnp.where(oh2, cum, 0.0), axis=-1, keepdims=True)
    run = run_scr[0:1, :] + jnp.sum(onehot, axis=0, keepdims=True)
    run_scr[...] = jnp.broadcast_to(run, run_scr.shape)
    cnt_ref[0] = jnp.broadcast_to(run, cnt_ref.shape[1:]).astype(jnp.int32)
    packed = jnp.where(lane == 0, i1, jnp.where(lane == 1, i2, jnp.where(lane == 2, rank1, jnp.where(lane == 3, rank2, 0.0))))
    idx_ref[...] = packed.astype(jnp.int32)
    wts_ref[...] = jnp.where(lane == 0, w1, jnp.where(lane == 1, w2, 0.0))


def _router(h, wr):
    m, d = h.shape
    tm = TM_ROUTE
    nt = m // tm
    return pl.pallas_call(
        _router_kernel, grid=(nt,),
        in_specs=[_rows(tm, d), _full((d, LANES))],
        out_specs=[_rows(tm, LANES), _rows(tm, LANES), pl.BlockSpec((1, 8, LANES), lambda i: (i, 0, 0))],
        out_shape=[jax.ShapeDtypeStruct((m, LANES), jnp.int32), jax.ShapeDtypeStruct((m, LANES), F32),
                   jax.ShapeDtypeStruct((nt, 8, LANES), jnp.int32)],
        scratch_shapes=[pltpu.VMEM((8, LANES), F32)],
        compiler_params=_cparams(1), name="router",
    )(h, wr)


def _row_copy(src, src_row, dst, dst_row, sem):
    return pltpu.make_async_copy(src.at[pl.ds(src_row, 1)], dst.at[pl.ds(dst_row, 1)], sem)


def _dispatch_kernel(slots_ref, h_ref, xs_in, xs_hbm, sem):
    del xs_in
    tm = h_ref.shape[0]

    def issue(r, c):
        _row_copy(h_ref, r, xs_hbm, slots_ref[0, 0, r], sem).start()
        _row_copy(h_ref, r, xs_hbm, slots_ref[0, 0, tm + r], sem).start()
        return c

    lax.fori_loop(0, tm, issue, 0)

    def drain(r, c):
        _row_copy(h_ref, 0, xs_hbm, 0, sem).wait()
        _row_copy(h_ref, 0, xs_hbm, 0, sem).wait()
        return c

    lax.fori_loop(0, tm, drain, 0)


def _dispatch(slots, h, n_slots):
    m, d = h.shape
    nt = slots.shape[0]
    xs0 = jnp.zeros((n_slots, d), F32)
    return pl.pallas_call(
        _dispatch_kernel, grid=(nt,),
        in_specs=[pl.BlockSpec((1, 1, slots.shape[2]), lambda i: (i, 0, 0), memory_space=pltpu.SMEM),
                  _rows(m // nt, d), pl.BlockSpec(memory_space=pl.ANY)],
        out_specs=pl.BlockSpec(memory_space=pl.ANY),
        out_shape=jax.ShapeDtypeStruct((n_slots, d), F32),
        scratch_shapes=[pltpu.SemaphoreType.DMA(())],
        input_output_aliases={2: 0},
        compiler_params=_cparams(1), name="moe_dispatch",
    )(slots, h, xs0)


def _combine_kernel(slots_ref, x_ref, wts_ref, ys_hbm, gn_ref, xn_ref, hn_ref, buf, sem):
    tm = x_ref.shape[0]

    def issue(r, c):
        _row_copy(ys_hbm, slots_ref[0, 0, r], buf.at[0], r, sem).start()
        _row_copy(ys_hbm, slots_ref[0, 0, tm + r], buf.at[1], r, sem).start()
        return c

    lax.fori_loop(0, tm, issue, 0)

    def drain(r, c):
        _row_copy(ys_hbm, 0, buf.at[0], 0, sem).wait()
        _row_copy(ys_hbm, 0, buf.at[1], 0, sem).wait()
        return c

    lax.fori_loop(0, tm, drain, 0)
    wts = wts_ref[...]
    xn = x_ref[...] + wts[:, 0:1] * buf[0] + wts[:, 1:2] * buf[1]
    xn_ref[...] = xn
    hn_ref[...] = _rms(xn, gn_ref[...])


def _combine(slots, x, wts, ys, gn):
    m, d = x.shape
    tm = TM_ROUTE
    return pl.pallas_call(
        _combine_kernel, grid=(m // tm,),
        in_specs=[pl.BlockSpec((1, 1, slots.shape[2]), lambda i: (i, 0, 0), memory_space=pltpu.SMEM),
                  _rows(tm, d), _rows(tm, LANES), pl.BlockSpec(memory_space=pl.ANY), _full((1, d))],
        out_specs=[_rows(tm, d)] * 2,
        out_shape=[jax.ShapeDtypeStruct((m, d), F32)] * 2,
        scratch_shapes=[pltpu.VMEM((2, tm, d), F32), pltpu.SemaphoreType.DMA(())],
        compiler_params=_cparams(1), name="moe_combine",
    )(slots, x, wts, ys, gn)


def _moe(x, h, router, wg, wu, wd, gn):
    m, d = x.shape
    tg = TM_FFN
    tm = TM_ROUTE
    wr = jnp.zeros((d, LANES), F32).at[:, :N_EXPERTS].set(router)
    idx, wts, cnt = _router(h, wr)
    counts = cnt[-1, 0, :N_EXPERTS]
    padded = ((counts + tg - 1) // tg) * tg
    ends = jnp.cumsum(padded)
    offs = ends - padded
    n_tiles = (2 * m) // tg + N_EXPERTS
    expert_ids = jnp.arange(N_EXPERTS, dtype=jnp.int32)[None, :]
    slot1 = jnp.sum(jnp.where(idx[:, 0:1] == expert_ids, offs[None, :], 0), axis=1) + idx[:, 2]
    slot2 = jnp.sum(jnp.where(idx[:, 1:2] == expert_ids, offs[None, :], 0), axis=1) + idx[:, 3]
    slots = jnp.concatenate([slot1.reshape(m // tm, 1, tm), slot2.reshape(m // tm, 1, tm)], axis=2).astype(jnp.int32)
    starts = jnp.arange(n_tiles, dtype=jnp.int32) * tg
    tile_expert = jnp.minimum(jnp.sum(ends[None, :] <= starts[:, None], axis=1), N_EXPERTS - 1)
    te = jnp.concatenate([tile_expert, ends[-1:] // tg]).astype(jnp.int32)
    xs = _dispatch(slots, h, n_tiles * tg)
    ys = _moe_ffn(te, xs, wg, wu, wd)
    return _combine(slots, x, wts, ys, gn)


def _pad_lora(a, b):
    rnk = a.shape[1]
    rp = -(-rnk // LANES) * LANES
    return jnp.pad(a, ((0, 0), (0, rp - rnk))), jnp.pad(b, ((0, rp - rnk), (0, 0)))


def kernel(x_prompt, x_sample, state_shift, state_wkv, norm_mix_g, norm_ffn_g, norm_out_g, a_w_in, a_b_in, a_ln_g, a_ln_b, a_w_s, a_b_s, a_w_out, b_mu, b_w_rkv, b_w0, b_w1, b_w2, b_a0, b_a1, b_a2, b_v0, b_v1, b_v2, b_g1, b_g2, b_k_k, b_k_a, b_r_k, b_lnx_g, b_lnx_b, b_w_out, f_w_gate, f_w_up, f_w_down, m_router, m_w_gate, m_w_up, m_w_down):
    bp, tp, d = x_prompt.shape
    bs, ts, _ = x_sample.shape
    depth = norm_mix_g.shape[0]
    n_heads = d // HEAD_DIM
    mp = bp * tp
    ms = bs * ts
    aw = a_w_out.shape[1]
    gd = aw // A_GROUPS
    row = lambda a: a.reshape(1, -1)

    nq = n_heads // 4
    next_gain = lambda i, sub: row(norm_ffn_g[i]) if sub == 0 else row(norm_mix_g[i + 1] if i + 1 < depth else norm_out_g)

    head_of = jnp.arange(d) // HEAD_DIM
    e_mat = (head_of[:, None] == jnp.arange(LANES)[None, :]).astype(BF16)
    et_mat = e_mat.T

    def layer_a(x, h, i, n_prompt_rows, precise):
        j = i // 2
        wdt = _wdtype(precise)
        w_in = a_w_in[j].astype(wdt)
        tri = jnp.tril(jnp.ones((CHUNK, CHUNK), bool))
        wc_p = jnp.where(tri, a_w_s[j], 0.0)
        rr = jnp.arange(CHUNK)
        same_seq = (rr[:, None] // ts) == (rr[None, :] // ts)
        wc_s = jnp.where(same_seq & tri, a_w_s[j][:, rr[:, None] % ts, rr[None, :] % ts], 0.0)
        wc2 = jnp.stack([wc_p, wc_s]).astype(wdt)
        bs_p = jnp.repeat(a_b_s[j].T, gd, axis=1)
        bs_s = jnp.repeat(a_b_s[j][:, rr % ts].T, gd, axis=1)
        bs2 = jnp.stack([bs_p, bs_s])
        x, h, vn = _sgu(x, h, w_in[:, :aw], w_in[:, aw:], row(a_b_in[j][:aw]), row(a_b_in[j][aw:]),
                        row(a_ln_g[j]), row(a_ln_b[j]), wc2, bs2, a_w_out[j].astype(wdt),
                        next_gain(i, 0), n_prompt_rows, precise)
        x, h = _ffn(x, h, f_w_gate[j].astype(wdt), f_w_up[j].astype(wdt), f_w_down[j].astype(wdt),
                    next_gain(i, 1), precise)
        return x, h, vn

    def tile_state(s):
        n = s.shape[0]
        return s.reshape(n, nq, 4, HEAD_DIM, HEAD_DIM).transpose(0, 1, 3, 2, 4).reshape(n, nq, HEAD_DIM, 4 * HEAD_DIM)

    def untile_state(s):
        n = s.shape[0]
        return s.reshape(n, nq, HEAD_DIM, 4, HEAD_DIM).transpose(0, 1, 3, 2, 4).reshape(n, n_heads, HEAD_DIM, HEAD_DIM)

    def rwkv_mixer(x, h, i, n_p_rows, n_s_rows, v_first, precise):
        j = i // 2
        wdt = _wdtype(precise)
        prev, shifts = [], {}
        if n_p_rows:
            hp3 = h[:n_p_rows].reshape(bp, tp, d)
            shifts["p"] = hp3[:, -1]
            prev.append(jnp.concatenate([jnp.zeros((bp, 1, d), F32), hp3[:, :-1]], axis=1).reshape(n_p_rows, d))
        if n_s_rows:
            hs3 = h[n_p_rows:].reshape(bs, ts, d)
            shifts["s"] = hs3[:, -1]
            prev.append(jnp.concatenate([state_shift[j][:, None, :], hs3[:, :-1]], axis=1).reshape(n_s_rows, d))
        hprev = prev[0] if len(prev) == 1 else jnp.concatenate(prev, axis=0)
        lora = lambda a, b: tuple(z.astype(wdt) for z in _pad_lora(a, b))
        w1, w2 = lora(b_w1[j], b_w2[j])
        a1, a2 = lora(b_a1[j], b_a2[j])
        g1, g2 = lora(b_g1[j], b_g2[j])
        vres_args = None
        if j > 0:
            v1, v2 = lora(b_v1[j - 1], b_v2[j - 1])
            vres_args = (row(b_v0[j - 1]), v1, v2, v_first)
        r, w, k, v, kk, kka, g = _rwkv_proj(
            h, hprev, b_mu[j], b_w_rkv[j].astype(wdt), row(b_w0[j]), w1, w2, row(b_a0[j]), a1, a2, g1, g2,
            row(b_k_k[j]), row(b_k_a[j]), e_mat, et_mat, vres_args, precise)
        ops = [kk, w, kka, k, r, v]
        ys, states = [], {}
        if n_p_rows:
            y_p, st_p = _scan_prompt(ops, bp, tp)
            ys.append(y_p.reshape(n_p_rows, d))
            states["p"] = untile_state(st_p)
        if n_s_rows:
            y_s, st_s = _scan_sample(ops, tile_state(state_wkv[j]), n_p_rows, bs, ts, precise)
            ys.append(y_s)
            states["s"] = untile_state(st_s)
        y = ys[0] if len(ys) == 1 else jnp.concatenate(ys, axis=0)
        x, h = _rwkv_out(x, y, r, k, v, g, row(b_r_k[j].reshape(-1)), row(b_lnx_g[j]), row(b_lnx_b[j]),
                         e_mat, et_mat, b_w_out[j].astype(wdt), next_gain(i, 0), precise)
        return x, h, v, shifts, states

    chunk_v, shift_p, wkv_p, shift_s, wkv_s = [], [], [], [], []
    xp = x_prompt.reshape(mp, d)
    xs = x_sample.reshape(ms, d)
    hp = _norm(xp, row(norm_mix_g[0]))
    hs = _norm(xs, row(norm_mix_g[0]))
    xp, hp, _ = layer_a(xp, hp, 0, mp, False)
    xs, hs, vn_s = layer_a(xs, hs, 0, 0, True)
    chunk_v.append(vn_s.reshape(bs, ts, aw))
    v_first = None
    if depth > 1:
        xp, hp, v_p, sh, st = rwkv_mixer(xp, hp, 1, mp, 0, None, False)
        shift_p.append(sh["p"])
        wkv_p.append(st["p"])
        xs, hs, v_s, sh, st = rwkv_mixer(xs, hs, 1, 0, ms, None, True)
        shift_s.append(sh["s"])
        wkv_s.append(st["s"])
        v_first = jnp.concatenate([v_p, v_s], axis=0)
    x = jnp.concatenate([xp, xs], axis=0)
    h = jnp.concatenate([hp, hs], axis=0)
    for i in range(1, depth):
        j = i // 2
        if i % 2 == 0:
            x, h, vn = layer_a(x, h, i, mp, False)
            chunk_v.append(vn[mp:].reshape(bs, ts, aw))
        else:
            if i > 1:
                x, h, _, sh, st = rwkv_mixer(x, h, i, mp, ms, v_first, False)
                shift_p.append(sh["p"])
                shift_s.append(sh["s"])
                wkv_p.append(st["p"])
                wkv_s.append(st["s"])
            x, h = _moe(x, h, m_router[j], m_w_gate[j].astype(BF16), m_w_up[j].astype(BF16),
                        m_w_down[j].astype(BF16), next_gain(i, 1))
    y_prompt = h[:mp].reshape(bp, tp, d)
    y_sample = h[mp:].reshape(bs, ts, d)
    return (y_prompt, y_sample, jnp.stack(chunk_v), jnp.stack(shift_p), jnp.stack(wkv_p),
            jnp.stack(shift_s), jnp.stack(wkv_s))
```

```python
import functools

import jax
import jax.numpy as jnp
from jax import lax
from jax.experimental import pallas as pl
from jax.experimental.pallas import tpu as pltpu

F32 = jnp.float32
BF16 = jnp.bfloat16

RMS_EPS = 1e-6
LN_EPS = 1e-5
GN_EPS = 64e-5
HEAD_DIM = 64
CHUNK = 128
A_GROUPS = 8
N_EXPERTS = 8
LANES = 128
MXU_DIM = 256
VMEM_LIMIT = 56 * 1024 * 1024

TM_SGU = 256
TM_FFN = 512
TM_RWKV = 256
TM_ROUTE = 512
SCAN_SEQS = 2
SCAN_STEPS = 256
SCAN_GROUP = 8
SCAN_PIECES = (1, 1, 1)
SCAN_PIECES_PRECISE = (3, 3, 3)


def _cparams(n_axes):
    return pltpu.CompilerParams(dimension_semantics=("arbitrary",) * n_axes,
                                vmem_limit_bytes=VMEM_LIMIT)


def _dot(a, b):
    return jnp.dot(a, b, preferred_element_type=F32)


def _mm(a, w, precise):
    if precise:
        return jnp.dot(a.astype(F32), w, preferred_element_type=F32, precision=lax.Precision.HIGHEST)
    return jnp.dot(a.astype(BF16), w, preferred_element_type=F32)


def _split(x, n):
    pieces = []
    for _ in range(n - 1):
        p = x.astype(BF16)
        pieces.append(p)
        x = x - p.astype(F32)
    pieces.append(x.astype(BF16))
    return pieces


def _wdtype(precise):
    return F32 if precise else BF16


def _add_all(terms):
    return functools.reduce(lambda a, b: a + b, terms)


def _rms(x, g):
    ms = jnp.mean(x * x, axis=-1, keepdims=True)
    return x * lax.rsqrt(ms + RMS_EPS) * g


def _gelu(z):
    return 0.5 * z * (1.0 + lax.erf(z * 0.7071067811865476))


def _full(shape, single=False):
    n = len(shape)
    if single:
        return pl.BlockSpec(shape, lambda *_: (0,) * n, pipeline_mode=pl.Buffered(1))
    return pl.BlockSpec(shape, lambda *_: (0,) * n)


def _rows(tm, d):
    return pl.BlockSpec((tm, d), lambda i, *_: (i, 0))


def _sgu_kernel(x_ref, h_ref, wu_ref, wv_ref, bu_ref, bv_ref, lng_ref, lnb_ref, wc_ref, bs_ref,
                wout_ref, gn_ref, xn_ref, hn_ref, vn_ref, gated_scr, *, precise, norm_input):
    tm = x_ref.shape[0]
    aw = vn_ref.shape[1]
    gd = aw // A_GROUPS
    h = _rms(x_ref[...], h_ref[...]) if norm_input else h_ref[...]
    hb = h.astype(gated_scr.dtype)
    zv = _gelu(_mm(hb, wv_ref[...], precise) + bv_ref[...])
    mu = jnp.mean(zv, axis=-1, keepdims=True)
    zc = zv - mu
    var = jnp.mean(zc * zc, axis=-1, keepdims=True)
    vn_ref[...] = zc * lax.rsqrt(var + LN_EPS) * lng_ref[...] + lnb_ref[...]
    for g in range(A_GROUPS):
        cs = slice(g * gd, (g + 1) * gd)
        zu = _gelu(_mm(hb, wu_ref[:, cs], precise) + bu_ref[:, cs])
        for c in range(tm // CHUNK):
            rs = slice(c * CHUNK, (c + 1) * CHUNK)
            vn = vn_ref[rs, cs].astype(gated_scr.dtype)
            mixed = _mm(wc_ref[0, g], vn, precise) + bs_ref[0, :, cs]
            gated_scr[rs, cs] = (zu[rs] * mixed).astype(gated_scr.dtype)
    xn = x_ref[...] + _mm(gated_scr[...], wout_ref[...], precise)
    xn_ref[...] = xn
    hn_ref[...] = _rms(xn, gn_ref[...])


def _sgu(x, h, wu, wv, bu, bv, lng, lnb, wc2, bs2, wout, gn, n_prompt_rows, precise=False):
    m, d = x.shape
    aw = wu.shape[1]
    tm = TM_SGU
    n_prompt_tiles = n_prompt_rows // tm
    norm_input = h.shape[0] == 1
    sel4 = lambda i: (jnp.where(i >= n_prompt_tiles, 1, 0), 0, 0, 0)
    sel3 = lambda i: (jnp.where(i >= n_prompt_tiles, 1, 0), 0, 0)
    return pl.pallas_call(
        functools.partial(_sgu_kernel, precise=precise, norm_input=norm_input), grid=(m // tm,),
        in_specs=[_rows(tm, d), _full((1, d)) if norm_input else _rows(tm, d),
                  _full((d, aw), precise), _full((d, aw), precise),
                  _full((1, aw)), _full((1, aw)), _full((1, aw)), _full((1, aw)),
                  pl.BlockSpec((1, A_GROUPS, CHUNK, CHUNK), sel4),
                  pl.BlockSpec((1, CHUNK, aw), sel3),
                  _full((aw, d), precise), _full((1, d))],
        out_specs=[_rows(tm, d), _rows(tm, d), _rows(tm, aw)],
        out_shape=[jax.ShapeDtypeStruct((m, d), F32), jax.ShapeDtypeStruct((m, d), F32),
                   jax.ShapeDtypeStruct((m, aw), F32)],
        scratch_shapes=[pltpu.VMEM((tm, aw), _wdtype(precise))],
        compiler_params=_cparams(1), name="sgu_precise" if precise else "sgu",
    )(x, h, wu, wv, bu, bv, lng, lnb, wc2, bs2, wout, gn)


def _swiglu_part(hb, wg_ref, wu_ref, wd_ref, precise=False):
    a = _mm(hb, wg_ref[...], precise)
    b = _mm(hb, wu_ref[...], precise)
    return _mm(a * jax.nn.sigmoid(a) * b, wd_ref[...], precise)


def _ffn_kernel(x_ref, h_ref, wg_ref, wu_ref, wd_ref, gn_ref, xn_ref, hn_ref, hb_scr, acc_scr, *, precise):
    k = pl.program_id(1)

    @pl.when(k == 0)
    def _():
        hb_scr[...] = h_ref[...].astype(hb_scr.dtype)

    part = _swiglu_part(hb_scr[...], wg_ref, wu_ref, wd_ref, precise)

    @pl.when(k == 0)
    def _():
        acc_scr[...] = part

    @pl.when(k > 0)
    def _():
        acc_scr[...] += part

    @pl.when(k == pl.num_programs(1) - 1)
    def _():
        xn = x_ref[...] + acc_scr[...]
        xn_ref[...] = xn
        hn_ref[...] = _rms(xn, gn_ref[...])


def _ffn_tiles(f, precise=False):
    if precise and f % MXU_DIM == 0:
        return f // MXU_DIM, MXU_DIM
    nk = 2 if f % (2 * LANES) == 0 else 1
    return nk, f // nk


def _ffn(x, h, wg, wu, wd, gn, precise=False):
    m, d = x.shape
    f = wg.shape[1]
    nk, tf = _ffn_tiles(f, precise)
    tm = TM_FFN
    return pl.pallas_call(
        functools.partial(_ffn_kernel, precise=precise), grid=(m // tm, nk),
        in_specs=[_rows(tm, d), _rows(tm, d),
                  pl.BlockSpec((d, tf), lambda i, k: (0, k)), pl.BlockSpec((d, tf), lambda i, k: (0, k)),
                  pl.BlockSpec((tf, d), lambda i, k: (k, 0)), _full((1, d))],
        out_specs=[_rows(tm, d), _rows(tm, d)],
        out_shape=[jax.ShapeDtypeStruct((m, d), F32), jax.ShapeDtypeStruct((m, d), F32)],
        scratch_shapes=[pltpu.VMEM((tm, d), _wdtype(precise)), pltpu.VMEM((tm, d), F32)],
        compiler_params=_cparams(2), name="swiglu_precise" if precise else "swiglu",
    )(x, h, wg, wu, wd, gn)


def _moe_ffn_kernel(te_ref, xs_ref, wg_ref, wu_ref, wd_ref, ys_ref, hb_scr, acc_scr):
    i = pl.program_id(0)
    k = pl.program_id(1)
    nk = pl.num_programs(1)
    n_used = te_ref[pl.num_programs(0)]

    @pl.when(i < n_used)
    def _():
        @pl.when(k == 0)
        def _():
            hb_scr[...] = xs_ref[...].astype(BF16)

        part = _swiglu_part(hb_scr[...], wg_ref, wu_ref, wd_ref)

        @pl.when(k == 0)
        def _():
            acc_scr[...] = part

        @pl.when(k > 0)
        def _():
            acc_scr[...] += part

        @pl.when(k == nk - 1)
        def _():
            ys_ref[...] = acc_scr[...]

    @pl.when((i >= n_used) & (k == nk - 1))
    def _():
        ys_ref[...] = jnp.zeros_like(ys_ref)


def _moe_ffn(te, xs, wg, wu, wd):
    s, d = xs.shape
    f = wg.shape[2]
    nk, tf = _ffn_tiles(f)
    tg = TM_FFN
    grid_spec = pltpu.PrefetchScalarGridSpec(
        num_scalar_prefetch=1, grid=(s // tg, nk),
        in_specs=[pl.BlockSpec((tg, d), lambda i, k, te: (i, 0)),
                  pl.BlockSpec((None, d, tf), lambda i, k, te: (te[i], 0, k)),
                  pl.BlockSpec((None, d, tf), lambda i, k, te: (te[i], 0, k)),
                  pl.BlockSpec((None, tf, d), lambda i, k, te: (te[i], k, 0))],
        out_specs=pl.BlockSpec((tg, d), lambda i, k, te: (i, 0)),
        scratch_shapes=[pltpu.VMEM((tg, d), BF16), pltpu.VMEM((tg, d), F32)])
    return pl.pallas_call(
        _moe_ffn_kernel, grid_spec=grid_spec,
        out_shape=jax.ShapeDtypeStruct((s, d), F32),
        compiler_params=_cparams(2), name="moe_swiglu",
    )(te, xs, wg, wu, wd)


def _head_sum(x, e_ref, et_ref, precise=False):
    e = e_ref[...]
    et = et_ref[...]
    s = _add_all([_dot(p, e) for p in _split(x, 3 if precise else 1)])
    return _add_all([_dot(p, et) for p in _split(s, 3)])


def _softplus(z):
    return jnp.maximum(z, 0.0) + jnp.log1p(jnp.exp(-jnp.abs(z)))


def _rwkv_proj_kernel(*refs, vres, precise, seq_len, n_shift_tiles):
    if vres:
        (h_ref, h8_ref, hps_ref, mu_ref, wrkv_ref, w0_ref, w1_ref, w2_ref, a0_ref, a1_ref, a2_ref, g1_ref, g2_ref,
         kk_ref_, ka_ref, e_ref, et_ref, v0_ref, v1_ref, v2_ref, vf_ref,
         r_o, w_o, k_o, v_o, kk_o, kka_o, g_o) = refs
    else:
        (h_ref, h8_ref, hps_ref, mu_ref, wrkv_ref, w0_ref, w1_ref, w2_ref, a0_ref, a1_ref, a2_ref, g1_ref, g2_ref,
         kk_ref_, ka_ref, e_ref, et_ref,
         r_o, w_o, k_o, v_o, kk_o, kka_o, g_o) = refs
    i = pl.program_id(0)
    tm = h_ref.shape[0]
    h = h_ref[...]
    before = jnp.where((i * tm) % seq_len == 0, 0.0, h8_ref[h8_ref.shape[0] - 1:, :])
    shifted = jnp.concatenate([before, h[:tm - 1]], axis=0)
    hprev = jnp.where(i < n_shift_tiles, shifted, hps_ref[...])
    xx = hprev - h

    mm = functools.partial(_mm, precise=precise)

    def mix(s):
        return (h + xx * mu_ref[s:s + 1, :]).astype(_wdtype(precise))

    r_o[...] = mm(mix(0), wrkv_ref[0])
    k = mm(mix(1), wrkv_ref[1])
    xv = mix(2)
    v = mm(xv, wrkv_ref[2])
    wl = w0_ref[...] + mm(jnp.tanh(mm(mix(3), w1_ref[...])), w2_ref[...])
    w_log = -_softplus(-wl) - 0.5
    w_o[...] = jnp.exp(-jnp.exp(w_log))
    alpha = jax.nn.sigmoid(a0_ref[...] + mm(mm(mix(4), a1_ref[...]), a2_ref[...]))
    if vres:
        gate = jax.nn.sigmoid(v0_ref[...] + mm(mm(xv, v1_ref[...]), v2_ref[...]))
        v = v + (vf_ref[...] - v) * gate
    v_o[...] = v
    g_o[...] = mm(jax.nn.sigmoid(mm(mix(5), g1_ref[...])), g2_ref[...])
    kk = k * kk_ref_[...]
    nrm = jnp.maximum(jnp.sqrt(_head_sum(kk * kk, e_ref, et_ref, precise)), 1e-12)
    kk = kk / nrm
    kk_o[...] = kk
    kka_o[...] = kk * alpha
    k_o[...] = k * (1.0 + (alpha - 1.0) * ka_ref[...])


def _rwkv_proj(h, hps, n_shift_rows, seq_len, mu, wrkv, w0, w1, w2, a0, a1, a2, g1, g2, k_k, k_a, e, et,
               vres_args, precise=False):
    m, d = h.shape
    tm = TM_RWKV
    sub = 8
    assert seq_len % tm == 0 and n_shift_rows % tm == 0
    n_shift_tiles = n_shift_rows // tm
    vres = vres_args is not None
    args = [h, h, hps, mu, wrkv, w0, w1, w2, a0, a1, a2, g1, g2, k_k, k_a, e, et]
    specs = [_rows(tm, d),
             pl.BlockSpec((sub, d), lambda i: (jnp.maximum(i * (tm // sub) - 1, 0), 0)),
             pl.BlockSpec((tm, d), lambda i: (jnp.maximum(i - n_shift_tiles, 0), 0))]
    specs += [_full(a.shape, precise) for a in args[3:]]
    if vres:
        v0, v1, v2, vf = vres_args
        args += [v0, v1, v2, vf]
        specs += [_full(v0.shape), _full(v1.shape), _full(v2.shape), _rows(tm, d)]
    return pl.pallas_call(
        functools.partial(_rwkv_proj_kernel, vres=vres, precise=precise, seq_len=seq_len,
                          n_shift_tiles=n_shift_tiles), grid=(m // tm,),
        in_specs=specs,
        out_specs=[_rows(tm, d)] * 7,
        out_shape=[jax.ShapeDtypeStruct((m, d), F32)] * 7,
        compiler_params=_cparams(1), name="rwkv_proj_precise" if precise else "rwkv_proj",
    )(*args)


N_SCAN_OPS = 6


def _scan_kernel(*refs, nb, nq, prompt, seq_steps, pieces):
    n_in = N_SCAN_OPS * (nb if prompt else 1)
    op_refs = refs[:n_in]
    if prompt:
        y_ref, st_ref = refs[n_in:n_in + 2]
        st_scrs = refs[n_in + 2:]
    else:
        s0_ref, y_ref, st_ref = refs[n_in:n_in + 3]
        st_scrs = refs[n_in + 3:]
    width = 4 * HEAD_DIM
    if prompt:
        @pl.when(pl.program_id(1) == 0)
        def _():
            for s_scr in st_scrs:
                s_scr[...] = jnp.zeros_like(s_scr)
    else:
        for s in range(nb):
            for q in range(nq):
                st_scrs[s * nq + q][...] = s0_ref[s, q]
    sub = lax.broadcasted_iota(jnp.int32, (HEAD_DIM, width), 0)
    lane = lax.broadcasted_iota(jnp.int32, (HEAD_DIM, width), 1)
    dmask = sub == (lane % HEAD_DIM)
    rr = lax.broadcasted_iota(jnp.int32, (width, width), 0)
    cc = lax.broadcasted_iota(jnp.int32, (width, width), 1)
    ones_blk = ((rr // HEAD_DIM) == (cc // HEAD_DIM)).astype(BF16)
    n_p, n_v, n_q = pieces
    n_lhs = (n_p + n_v) * HEAD_DIM

    def blocks_sum(o, start, n):
        return _add_all([o[start + i * HEAD_DIM:start + (i + 1) * HEAD_DIM] for i in range(n)])

    def y_row(yb):
        rows_q = nq * HEAD_DIM
        yb = _add_all([yb[i * rows_q:(i + 1) * rows_q] for i in range(n_q)])
        return jnp.concatenate(
            [jnp.sum(jnp.where(dmask, yb[q * HEAD_DIM:(q + 1) * HEAD_DIM], 0.0), axis=0, keepdims=True)
             for q in range(nq)], axis=1)

    def run_group(row, n_steps):
        ys = [[None] * n_steps for _ in range(nb)]
        q_prev = [None] * nb
        for u in range(n_steps):
            for s in range(nb):
                st = [st_scrs[s * nq + q][...] for q in range(nq)]
                lhs = []
                for q in range(nq):
                    lhs += _split(st[q] * row(0, s, u, q), n_p)
                    lhs += _split(jnp.where(dmask, row(5, s, u, q), 0.0), n_v)
                if u > 0:
                    lhs.append(q_prev[s])
                o = _dot(jnp.concatenate(lhs, axis=0), ones_blk)
                qs = [[] for _ in range(n_q)]
                for q in range(nq):
                    skk = blocks_sum(o, q * n_lhs, n_p)
                    vb = blocks_sum(o, q * n_lhs + n_p * HEAD_DIM, n_v)
                    sn = st[q] * row(1, s, u, q) - skk * row(2, s, u, q) + vb * row(3, s, u, q)
                    st_scrs[s * nq + q][...] = sn
                    for i, piece in enumerate(_split(sn * row(4, s, u, q), n_q)):
                        qs[i].append(piece)
                if u > 0:
                    ys[s][u - 1] = y_row(o[nq * n_lhs:])
                q_prev[s] = jnp.concatenate([p for grp in qs for p in grp], axis=0)
        for s in range(nb):
            ys[s][n_steps - 1] = y_row(_dot(q_prev[s], ones_blk))
        return ys

    if prompt:
        def group(g, c):
            r0 = pl.multiple_of(g * SCAN_GROUP, SCAN_GROUP)
            x = [[op_refs[o * nb + s][pl.ds(r0, SCAN_GROUP), :] for s in range(nb)] for o in range(N_SCAN_OPS)]
            ys = run_group(lambda o, s, u, q: x[o][s][u:u + 1, q * width:(q + 1) * width], SCAN_GROUP)
            for s in range(nb):
                y_ref[s, pl.ds(r0, SCAN_GROUP), :] = jnp.concatenate(ys[s], axis=0)
            return c

        lax.fori_loop(0, op_refs[0].shape[0] // SCAN_GROUP, group, 0)
    else:
        x = [op_refs[o][...] for o in range(N_SCAN_OPS)]
        ys = run_group(lambda o, s, u, q: x[o][s * seq_steps + u:s * seq_steps + u + 1, q * width:(q + 1) * width],
                       seq_steps)
        y_ref[...] = jnp.concatenate([y for s in range(nb) for y in ys[s]], axis=0)

    def write_state():
        for s in range(nb):
            for q in range(nq):
                st_ref[s, q] = st_scrs[s * nq + q][...]

    if prompt:
        pl.when(pl.program_id(1) == pl.num_programs(1) - 1)(write_state)
    else:
        write_state()


def _scan_prompt(ops, n_seq, t_len):
    d = ops[0].shape[1]
    nb = SCAN_SEQS
    nq = d // (4 * HEAD_DIM)
    tb = min(t_len, SCAN_STEPS)
    n_chunks = t_len // tb
    in_specs, args = [], []
    for a in ops:
        for s in range(nb):
            in_specs.append(pl.BlockSpec((tb, d), lambda b, c, s=s: ((b * nb + s) * n_chunks + c, 0)))
            args.append(a)
    st_spec = pl.BlockSpec((nb, nq, HEAD_DIM, 4 * HEAD_DIM), lambda b, c: (b, 0, 0, 0))
    return pl.pallas_call(
        functools.partial(_scan_kernel, nb=nb, nq=nq, prompt=True, seq_steps=t_len, pieces=SCAN_PIECES),
        grid=(n_seq // nb, n_chunks),
        in_specs=in_specs,
        out_specs=[pl.BlockSpec((nb, tb, d), lambda b, c: (b, c, 0)), st_spec],
        out_shape=[jax.ShapeDtypeStruct((n_seq, t_len, d), F32),
                   jax.ShapeDtypeStruct((n_seq, nq, HEAD_DIM, 4 * HEAD_DIM), F32)],
        scratch_shapes=[pltpu.VMEM((HEAD_DIM, 4 * HEAD_DIM), F32)] * (nb * nq),
        compiler_params=_cparams(2), name="wkv_scan_prompt",
    )(*args)


def _scan_sample(ops, s0, row0, n_seq, t_len, precise=False):
    d = ops[0].shape[1]
    nq = d // (4 * HEAD_DIM)
    nb = SCAN_GROUP // t_len
    rows = nb * t_len
    blk0 = row0 // rows
    row_spec = pl.BlockSpec((rows, d), lambda b: (blk0 + b, 0))
    st_spec = pl.BlockSpec((nb, nq, HEAD_DIM, 4 * HEAD_DIM), lambda b: (b, 0, 0, 0))
    return pl.pallas_call(
        functools.partial(_scan_kernel, nb=nb, nq=nq, prompt=False, seq_steps=t_len,
                          pieces=SCAN_PIECES_PRECISE if precise else SCAN_PIECES),
        grid=(n_seq // nb,),
        in_specs=[row_spec] * N_SCAN_OPS + [st_spec],
        out_specs=[pl.BlockSpec((rows, d), lambda b: (b, 0)), st_spec],
        out_shape=[jax.ShapeDtypeStruct((n_seq * t_len, d), F32),
                   jax.ShapeDtypeStruct((n_seq, nq, HEAD_DIM, 4 * HEAD_DIM), F32)],
        scratch_shapes=[pltpu.VMEM((HEAD_DIM, 4 * HEAD_DIM), F32)] * (nb * nq),
        compiler_params=_cparams(1), name="wkv_scan_sample",
    )(*ops, s0)


def _rwkv_out_kernel(x_ref, y_ref, r_ref, k_ref, v_ref, g_ref, rk_ref, lg_ref, lb_ref, e_ref, et_ref,
                     wout_ref, gn_ref, xn_ref, hn_ref, *, precise):
    inv_n = 1.0 / HEAD_DIM
    hsum = functools.partial(_head_sum, e_ref=e_ref, et_ref=et_ref, precise=precise)
    y = y_ref[...]
    yc = y - hsum(y) * inv_n
    var = hsum(yc * yc) * inv_n
    yn = yc * lax.rsqrt(var + GN_EPS) * lg_ref[...] + lb_ref[...]
    bonus = hsum(r_ref[...] * k_ref[...] * rk_ref[...]) * v_ref[...]
    o = _mm((yn + bonus) * g_ref[...], wout_ref[...], precise)
    xn = x_ref[...] + o
    xn_ref[...] = xn
    hn_ref[...] = _rms(xn, gn_ref[...])


def _rwkv_out(x, y, r, k, v, g, rk, lg, lb, e, et, wout, gn, precise=False):
    m, d = x.shape
    tm = TM_RWKV
    small = [rk, lg, lb, e, et, wout, gn]
    return pl.pallas_call(
        functools.partial(_rwkv_out_kernel, precise=precise), grid=(m // tm,),
        in_specs=[_rows(tm, d)] * 6 + [_full(a.shape) for a in small],
        out_specs=[_rows(tm, d)] * 2,
        out_shape=[jax.ShapeDtypeStruct((m, d), F32)] * 2,
        compiler_params=_cparams(1), name="rwkv_out_precise" if precise else "rwkv_out",
    )(x, y, r, k, v, g, *small)


def _router_kernel(h_ref, wr_ref, idx_ref, wts_ref, cnt_ref, run_scr):
    tm = h_ref.shape[0]

    @pl.when(pl.program_id(0) == 0)
    def _():
        run_scr[...] = jnp.zeros_like(run_scr)

    lane = lax.broadcasted_iota(jnp.int32, (tm, LANES), 1)
    lanef = lane.astype(F32)
    logits = jnp.dot(h_ref[...], wr_ref[...], preferred_element_type=F32, precision=lax.Precision.HIGHEST)
    neg = jnp.float32(-jnp.inf)
    logits = jnp.where(lane < N_EXPERTS, logits, neg)
    m1 = jnp.max(logits, axis=-1, keepdims=True)
    i1 = jnp.min(jnp.where(logits == m1, lanef, float(LANES)), axis=-1, keepdims=True)
    l2 = jnp.where(lanef == i1, neg, logits)
    m2 = jnp.max(l2, axis=-1, keepdims=True)
    i2 = jnp.min(jnp.where(l2 == m2, lanef, float(LANES)), axis=-1, keepdims=True)
    ex = jnp.exp(m2 - m1)
    den = 1.0 + ex
    w1 = 1.0 / den
    w2 = ex / den
    oh1 = lanef == i1
    oh2 = lanef == i2
    onehot = jnp.where(oh1 | oh2, 1.0, 0.0)
    ri = lax.broadcasted_iota(jnp.int32, (tm, tm), 0)
    ci = lax.broadcasted_iota(jnp.int32, (tm, tm), 1)
    lower = (ci < ri).astype(BF16)
    cum = _dot(lower, onehot.astype(BF16)) + run_scr[0:1, :]
    rank1 = jnp.sum(jnp.where(oh1, cum, 0.0), axis=-1, keepdims=True)
    rank2 = jnp.sum(jnp.where(oh2, cum, 0.0), axis=-1, keepdims=True)
    run = run_scr[0:1, :] + jnp.sum(onehot, axis=0, keepdims=True)
    run_scr[...] = jnp.broadcast_to(run, run_scr.shape)
    cnt_ref[0] = jnp.broadcast_to(run, cnt_ref.shape[1:]).astype(jnp.int32)
    packed = jnp.where(lane == 0, i1, jnp.where(lane == 1, i2, jnp.where(lane == 2, rank1, jnp.where(lane == 3, rank2, 0.0))))
    idx_ref[...] = packed.astype(jnp.int32)
    wts_ref[...] = jnp.where(lane == 0, w1, jnp.where(lane == 1, w2, 0.0))


def _router(h, wr):
    m, d = h.shape
    tm = TM_ROUTE
    nt = m // tm
    return pl.pallas_call(
        _router_kernel, grid=(nt,),
        in_specs=[_rows(tm, d), _full((d, LANES))],
        out_specs=[_rows(tm, LANES), _rows(tm, LANES), pl.BlockSpec((1, 8, LANES), lambda i: (i, 0, 0))],
        out_shape=[jax.ShapeDtypeStruct((m, LANES), jnp.int32), jax.ShapeDtypeStruct((m, LANES), F32),
                   jax.ShapeDtypeStruct((nt, 8, LANES), jnp.int32)],
        scratch_shapes=[pltpu.VMEM((8, LANES), F32)],
        compiler_params=_cparams(1), name="router",
    )(h, wr)


def _row_copy(src, src_row, dst, dst_row, sem):
    return pltpu.make_async_copy(src.at[pl.ds(src_row, 1)], dst.at[pl.ds(dst_row, 1)], sem)


def _dispatch_kernel(slots_ref, h_ref, xs_in, xs_hbm, sem):
    del xs_in
    tm = h_ref.shape[0]

    def issue(r, c):
        _row_copy(h_ref, r, xs_hbm, slots_ref[0, 0, r], sem).start()
        _row_copy(h_ref, r, xs_hbm, slots_ref[0, 0, tm + r], sem).start()
        return c

    lax.fori_loop(0, tm, issue, 0)

    def drain(r, c):
        _row_copy(h_ref, 0, xs_hbm, 0, sem).wait()
        _row_copy(h_ref, 0, xs_hbm, 0, sem).wait()
        return c

    lax.fori_loop(0, tm, drain, 0)


def _dispatch(slots, h, n_slots):
    m, d = h.shape
    nt = slots.shape[0]
    xs0 = jnp.zeros((n_slots, d), F32)
    return pl.pallas_call(
        _dispatch_kernel, grid=(nt,),
        in_specs=[pl.BlockSpec((1, 1, slots.shape[2]), lambda i: (i, 0, 0), memory_space=pltpu.SMEM),
                  _rows(m // nt, d), pl.BlockSpec(memory_space=pl.ANY)],
        out_specs=pl.BlockSpec(memory_space=pl.ANY),
        out_shape=jax.ShapeDtypeStruct((n_slots, d), F32),
        scratch_shapes=[pltpu.SemaphoreType.DMA(())],
        input_output_aliases={2: 0},
        compiler_params=_cparams(1), name="moe_dispatch",
    )(slots, h, xs0)


def _combine_kernel(slots_ref, x_ref, wts_ref, ys_hbm, gn_ref, xn_ref, hn_ref, buf, sem):
    tm = x_ref.shape[0]

    def issue(r, c):
        _row_copy(ys_hbm, slots_ref[0, 0, r], buf.at[0], r, sem).start()
        _row_copy(ys_hbm, slots_ref[0, 0, tm + r], buf.at[1], r, sem).start()
        return c

    lax.fori_loop(0, tm, issue, 0)

    def drain(r, c):
        _row_copy(ys_hbm, 0, buf.at[0], 0, sem).wait()
        _row_copy(ys_hbm, 0, buf.at[1], 0, sem).wait()
        return c

    lax.fori_loop(0, tm, drain, 0)
    wts = wts_ref[...]
    xn = x_ref[...] + wts[:, 0:1] * buf[0] + wts[:, 1:2] * buf[1]
    xn_ref[...] = xn
    hn_ref[...] = _rms(xn, gn_ref[...])


def _combine(slots, x, wts, ys, gn):
    m, d = x.shape
    tm = TM_ROUTE
    return pl.pallas_call(
        _combine_kernel, grid=(m // tm,),
        in_specs=[pl.BlockSpec((1, 1, slots.shape[2]), lambda i: (i, 0, 0), memory_space=pltpu.SMEM),
                  _rows(tm, d), _rows(tm, LANES), pl.BlockSpec(memory_space=pl.ANY), _full((1, d))],
        out_specs=[_rows(tm, d)] * 2,
        out_shape=[jax.ShapeDtypeStruct((m, d), F32)] * 2,
        scratch_shapes=[pltpu.VMEM((2, tm, d), F32), pltpu.SemaphoreType.DMA(())],
        compiler_params=_cparams(1), name="moe_combine",
    )(slots, x, wts, ys, gn)


def _moe(x, h, router, wg, wu, wd, gn):
    m, d = x.shape
    tg = TM_FFN
    tm = TM_ROUTE
    wr = jnp.zeros((d, LANES), F32).at[:, :N_EXPERTS].set(router)
    idx, wts, cnt = _router(h, wr)
    counts = cnt[-1, 0, :N_EXPERTS]
    padded = ((counts + tg - 1) // tg) * tg
    ends = jnp.cumsum(padded)
    offs = ends - padded
    n_tiles = (2 * m) // tg + N_EXPERTS
    expert_ids = jnp.arange(N_EXPERTS, dtype=jnp.int32)[None, :]
    slot1 = jnp.sum(jnp.where(idx[:, 0:1] == expert_ids, offs[None, :], 0), axis=1) + idx[:, 2]
    slot2 = jnp.sum(jnp.where(idx[:, 1:2] == expert_ids, offs[None, :], 0), axis=1) + idx[:, 3]
    slots = jnp.concatenate([slot1.reshape(m // tm, 1, tm), slot2.reshape(m // tm, 1, tm)], axis=2).astype(jnp.int32)
    starts = jnp.arange(n_tiles, dtype=jnp.int32) * tg
    tile_expert = jnp.minimum(jnp.sum(ends[None, :] <= starts[:, None], axis=1), N_EXPERTS - 1)
    te = jnp.concatenate([tile_expert, ends[-1:] // tg]).astype(jnp.int32)
    xs = _dispatch(slots, h, n_tiles * tg)
    ys = _moe_ffn(te, xs, wg, wu, wd)
    return _combine(slots, x, wts, ys, gn)


def _pad_lora(a, b):
    rnk = a.shape[1]
    rp = -(-rnk // LANES) * LANES
    return jnp.pad(a, ((0, 0), (0, rp - rnk))), jnp.pad(b, ((0, rp - rnk), (0, 0)))


def kernel(x_prompt, x_sample, state_shift, state_wkv, norm_mix_g, norm_ffn_g, norm_out_g, a_w_in, a_b_in, a_ln_g, a_ln_b, a_w_s, a_b_s, a_w_out, b_mu, b_w_rkv, b_w0, b_w1, b_w2, b_a0, b_a1, b_a2, b_v0, b_v1, b_v2, b_g1, b_g2, b_k_k, b_k_a, b_r_k, b_lnx_g, b_lnx_b, b_w_out, f_w_gate, f_w_up, f_w_down, m_router, m_w_gate, m_w_up, m_w_down):
    bp, tp, d = x_prompt.shape
    bs, ts, _ = x_sample.shape
    depth = norm_mix_g.shape[0]
    n_heads = d // HEAD_DIM
    mp = bp * tp
    ms = bs * ts
    aw = a_w_out.shape[1]
    gd = aw // A_GROUPS
    row = lambda a: a.reshape(1, -1)

    nq = n_heads // 4
    next_gain = lambda i, sub: row(norm_ffn_g[i]) if sub == 0 else row(norm_mix_g[i + 1] if i + 1 < depth else norm_out_g)

    head_of = jnp.arange(d) // HEAD_DIM
    e_mat = (head_of[:, None] == jnp.arange(LANES)[None, :]).astype(BF16)
    et_mat = e_mat.T

    def layer_a(x, h, i, n_prompt_rows, precise):
        j = i // 2
        wdt = _wdtype(precise)
        w_in = a_w_in[j].astype(wdt)
        tri = jnp.tril(jnp.ones((CHUNK, CHUNK), bool))
        wc_p = jnp.where(tri, a_w_s[j], 0.0)
        rr = jnp.arange(CHUNK)
        same_seq = (rr[:, None] // ts) == (rr[None, :] // ts)
        reps = CHUNK // ts
        wc_s = jnp.where(same_seq & tri, jnp.tile(a_w_s[j][:, :ts, :ts], (1, reps, reps)), 0.0)
        wc2 = jnp.stack([wc_p, wc_s]).astype(wdt)
        bs_p = jnp.repeat(a_b_s[j].T, gd, axis=1)
        bs_s = jnp.repeat(jnp.tile(a_b_s[j][:, :ts], (1, reps)).T, gd, axis=1)
        bs2 = jnp.stack([bs_p, bs_s])
        x, h, vn = _sgu(x, h, w_in[:, :aw], w_in[:, aw:], row(a_b_in[j][:aw]), row(a_b_in[j][aw:]),
                        row(a_ln_g[j]), row(a_ln_b[j]), wc2, bs2, a_w_out[j].astype(wdt),
                        next_gain(i, 0), n_prompt_rows, precise)
        x, h = _ffn(x, h, f_w_gate[j].astype(wdt), f_w_up[j].astype(wdt), f_w_down[j].astype(wdt),
                    next_gain(i, 1), precise)
        return x, h, vn

    def tile_state(s):
        n = s.shape[0]
        return s.reshape(n, nq, 4, HEAD_DIM, HEAD_DIM).transpose(0, 1, 3, 2, 4).reshape(n, nq, HEAD_DIM, 4 * HEAD_DIM)

    def untile_state(s):
        n = s.shape[0]
        return s.reshape(n, nq, HEAD_DIM, 4, HEAD_DIM).transpose(0, 1, 3, 2, 4).reshape(n, n_heads, HEAD_DIM, HEAD_DIM)

    def rwkv_mixer(x, h, i, n_p_rows, n_s_rows, v_first, precise):
        j = i // 2
        wdt = _wdtype(precise)
        shifts = {}
        if n_p_rows:
            shifts["p"] = h[tp - 1:n_p_rows:tp]
        if n_s_rows:
            hs3 = h[n_p_rows:].reshape(bs, ts, d)
            shifts["s"] = hs3[:, -1]
            hps = jnp.concatenate([state_shift[j][:, None, :], hs3[:, :-1]], axis=1).reshape(n_s_rows, d)
        else:
            hps = jnp.zeros((TM_RWKV, d), F32)
        lora = lambda a, b: tuple(z.astype(wdt) for z in _pad_lora(a, b))
        w1, w2 = lora(b_w1[j], b_w2[j])
        a1, a2 = lora(b_a1[j], b_a2[j])
        g1, g2 = lora(b_g1[j], b_g2[j])
        vres_args = None
        if j > 0:
            v1, v2 = lora(b_v1[j - 1], b_v2[j - 1])
            vres_args = (row(b_v0[j - 1]), v1, v2, v_first)
        r, w, k, v, kk, kka, g = _rwkv_proj(
            h, hps, n_p_rows, tp, b_mu[j], b_w_rkv[j].astype(wdt), row(b_w0[j]), w1, w2, row(b_a0[j]), a1, a2,
            g1, g2, row(b_k_k[j]), row(b_k_a[j]), e_mat, et_mat, vres_args, precise)
        ops = [kk, w, kka, k, r, v]
        ys, states = [], {}
        if n_p_rows:
            y_p, st_p = _scan_prompt(ops, bp, tp)
            ys.append(y_p.reshape(n_p_rows, d))
            states["p"] = untile_state(st_p)
        if n_s_rows:
            y_s, st_s = _scan_sample(ops, tile_state(state_wkv[j]), n_p_rows, bs, ts, precise)
            ys.append(y_s)
            states["s"] = untile_state(st_s)
        y = ys[0] if len(ys) == 1 else jnp.concatenate(ys, axis=0)
        x, h = _rwkv_out(x, y, r, k, v, g, row(b_r_k[j].reshape(-1)), row(b_lnx_g[j]), row(b_lnx_b[j]),
                         e_mat, et_mat, b_w_out[j].astype(wdt), next_gain(i, 0), precise)
        return x, h, v, shifts, states

    chunk_v, shift_p, wkv_p, shift_s, wkv_s = [], [], [], [], []
    xp = x_prompt.reshape(mp, d)
    xs = x_sample.reshape(ms, d)
    xp, hp, _ = layer_a(xp, row(norm_mix_g[0]), 0, mp, False)
    xs, hs, vn_s = layer_a(xs, row(norm_mix_g[0]), 0, 0, True)
    chunk_v.append(vn_s.reshape(bs, ts, aw))
    v_first = None
    if depth > 1:
        xp, hp, v_p, sh, st = rwkv_mixer(xp, hp, 1, mp, 0, None, False)
        shift_p.append(sh["p"])
        wkv_p.append(st["p"])
        xs, hs, v_s, sh, st = rwkv_mixer(xs, hs, 1, 0, ms, None, True)
        shift_s.append(sh["s"])
        wkv_s.append(st["s"])
        v_first = jnp.concatenate([v_p, v_s], axis=0)
    x = jnp.concatenate([xp, xs], axis=0)
    h = jnp.concatenate([hp, hs], axis=0)
    for i in range(1, depth):
        j = i // 2
        if i % 2 == 0:
            x, h, vn = layer_a(x, h, i, mp, False)
            chunk_v.append(vn[mp:].reshape(bs, ts, aw))
        else:
            if i > 1:
                x, h, _, sh, st = rwkv_mixer(x, h, i, mp, ms, v_first, False)
                shift_p.append(sh["p"])
                shift_s.append(sh["s"])
                wkv_p.append(st["p"])
                wkv_s.append(st["s"])
            x, h = _moe(x, h, m_router[j], m_w_gate[j].astype(BF16), m_w_up[j].astype(BF16),
                        m_w_down[j].astype(BF16), next_gain(i, 1))
    y_prompt = h[:mp].reshape(bp, tp, d)
    y_sample = h[mp:].reshape(bs, ts, d)
    return (y_prompt, y_sample, jnp.stack(chunk_v), jnp.stack(shift_p), jnp.stack(wkv_p),
            jnp.stack(shift_s), jnp.stack(wkv_s))
```

```python
import functools

import jax
import jax.numpy as jnp
from jax import lax
from jax.experimental import pallas as pl
from jax.experimental.pallas import tpu as pltpu

F32 = jnp.float32
BF16 = jnp.bfloat16

RMS_EPS = 1e-6
LN_EPS = 1e-5
GN_EPS = 64e-5
HEAD_DIM = 64
CHUNK = 128
A_GROUPS = 8
N_EXPERTS = 8
LANES = 128
MXU_DIM = 256
VMEM_LIMIT = 56 * 1024 * 1024

TM_SGU = 256
TM_FFN = 512
TM_RWKV = 256
TM_ROUTE = 512
SCAN_SEQS = 4
SCAN_STEPS = 128
SCAN_SAMPLE_SEQS = 8
SCAN_GROUP = 8
SCAN_PIECES = (1, 1, 1)
SCAN_PIECES_PRECISE = (3, 3, 3)


def _cparams(n_axes):
    return pltpu.CompilerParams(dimension_semantics=("arbitrary",) * n_axes,
                                vmem_limit_bytes=VMEM_LIMIT)


def _dot(a, b):
    return jnp.dot(a, b, preferred_element_type=F32)


def _mm(a, w, precise):
    if precise:
        return jnp.dot(a.astype(F32), w, preferred_element_type=F32, precision=lax.Precision.HIGHEST)
    return jnp.dot(a.astype(BF16), w, preferred_element_type=F32)


def _split(x, n):
    pieces = []
    for _ in range(n - 1):
        p = x.astype(BF16)
        pieces.append(p)
        x = x - p.astype(F32)
    pieces.append(x.astype(BF16))
    return pieces


def _wdtype(precise):
    return F32 if precise else BF16


def _add_all(terms):
    return functools.reduce(lambda a, b: a + b, terms)


def _rms(x, g):
    ms = jnp.mean(x * x, axis=-1, keepdims=True)
    return x * lax.rsqrt(ms + RMS_EPS) * g


def _gelu(z):
    return 0.5 * z * (1.0 + lax.erf(z * 0.7071067811865476))


def _full(shape, single=False):
    n = len(shape)
    if single:
        return pl.BlockSpec(shape, lambda *_: (0,) * n, pipeline_mode=pl.Buffered(1))
    return pl.BlockSpec(shape, lambda *_: (0,) * n)


def _rows(tm, d):
    return pl.BlockSpec((tm, d), lambda i, *_: (i, 0))


def _sgu_kernel(x_ref, h_ref, wu_ref, wv_ref, bu_ref, bv_ref, lng_ref, lnb_ref, wc_ref, bs_ref,
                wout_ref, gn_ref, xn_ref, hn_ref, vn_ref, gated_scr, *, precise, norm_input):
    tm = x_ref.shape[0]
    aw = vn_ref.shape[1]
    gd = aw // A_GROUPS
    h = _rms(x_ref[...], h_ref[...]) if norm_input else h_ref[...]
    hb = h.astype(gated_scr.dtype)
    zv = _gelu(_mm(hb, wv_ref[...], precise) + bv_ref[...])
    mu = jnp.mean(zv, axis=-1, keepdims=True)
    zc = zv - mu
    var = jnp.mean(zc * zc, axis=-1, keepdims=True)
    vn_ref[...] = zc * lax.rsqrt(var + LN_EPS) * lng_ref[...] + lnb_ref[...]
    for g in range(A_GROUPS):
        cs = slice(g * gd, (g + 1) * gd)
        zu = _gelu(_mm(hb, wu_ref[:, cs], precise) + bu_ref[:, cs])
        for c in range(tm // CHUNK):
            rs = slice(c * CHUNK, (c + 1) * CHUNK)
            vn = vn_ref[rs, cs].astype(gated_scr.dtype)
            mixed = _mm(wc_ref[0, g], vn, precise) + bs_ref[0, :, cs]
            gated_scr[rs, cs] = (zu[rs] * mixed).astype(gated_scr.dtype)
    xn = x_ref[...] + _mm(gated_scr[...], wout_ref[...], precise)
    xn_ref[...] = xn
    hn_ref[...] = _rms(xn, gn_ref[...])


def _sgu(x, h, wu, wv, bu, bv, lng, lnb, wc2, bs2, wout, gn, n_prompt_rows, precise=False):
    m, d = x.shape
    aw = wu.shape[1]
    tm = TM_SGU
    n_prompt_tiles = n_prompt_rows // tm
    norm_input = h.shape[0] == 1
    sel4 = lambda i: (jnp.where(i >= n_prompt_tiles, 1, 0), 0, 0, 0)
    sel3 = lambda i: (jnp.where(i >= n_prompt_tiles, 1, 0), 0, 0)
    return pl.pallas_call(
        functools.partial(_sgu_kernel, precise=precise, norm_input=norm_input), grid=(m // tm,),
        in_specs=[_rows(tm, d), _full((1, d)) if norm_input else _rows(tm, d),
                  _full((d, aw), precise), _full((d, aw), precise),
                  _full((1, aw)), _full((1, aw)), _full((1, aw)), _full((1, aw)),
                  pl.BlockSpec((1, A_GROUPS, CHUNK, CHUNK), sel4),
                  pl.BlockSpec((1, CHUNK, aw), sel3),
                  _full((aw, d), precise), _full((1, d))],
        out_specs=[_rows(tm, d), _rows(tm, d), _rows(tm, aw)],
        out_shape=[jax.ShapeDtypeStruct((m, d), F32), jax.ShapeDtypeStruct((m, d), F32),
                   jax.ShapeDtypeStruct((m, aw), F32)],
        scratch_shapes=[pltpu.VMEM((tm, aw), _wdtype(precise))],
        compiler_params=_cparams(1), name="sgu_precise" if precise else "sgu",
    )(x, h, wu, wv, bu, bv, lng, lnb, wc2, bs2, wout, gn)


def _swiglu_part(hb, wg_ref, wu_ref, wd_ref, precise=False):
    a = _mm(hb, wg_ref[...], precise)
    b = _mm(hb, wu_ref[...], precise)
    return _mm(a * jax.nn.sigmoid(a) * b, wd_ref[...], precise)


def _ffn_kernel(x_ref, h_ref, wg_ref, wu_ref, wd_ref, gn_ref, xn_ref, hn_ref, hb_scr, acc_scr, *, precise):
    k = pl.program_id(1)

    @pl.when(k == 0)
    def _():
        hb_scr[...] = h_ref[...].astype(hb_scr.dtype)

    part = _swiglu_part(hb_scr[...], wg_ref, wu_ref, wd_ref, precise)

    @pl.when(k == 0)
    def _():
        acc_scr[...] = part

    @pl.when(k > 0)
    def _():
        acc_scr[...] += part

    @pl.when(k == pl.num_programs(1) - 1)
    def _():
        xn = x_ref[...] + acc_scr[...]
        xn_ref[...] = xn
        hn_ref[...] = _rms(xn, gn_ref[...])


def _ffn_tiles(f, precise=False):
    if precise and f % MXU_DIM == 0:
        return f // MXU_DIM, MXU_DIM
    nk = 2 if f % (2 * LANES) == 0 else 1
    return nk, f // nk


def _ffn(x, h, wg, wu, wd, gn, precise=False):
    m, d = x.shape
    f = wg.shape[1]
    nk, tf = _ffn_tiles(f, precise)
    tm = TM_FFN
    return pl.pallas_call(
        functools.partial(_ffn_kernel, precise=precise), grid=(m // tm, nk),
        in_specs=[_rows(tm, d), _rows(tm, d),
                  pl.BlockSpec((d, tf), lambda i, k: (0, k)), pl.BlockSpec((d, tf), lambda i, k: (0, k)),
                  pl.BlockSpec((tf, d), lambda i, k: (k, 0)), _full((1, d))],
        out_specs=[_rows(tm, d), _rows(tm, d)],
        out_shape=[jax.ShapeDtypeStruct((m, d), F32), jax.ShapeDtypeStruct((m, d), F32)],
        scratch_shapes=[pltpu.VMEM((tm, d), _wdtype(precise)), pltpu.VMEM((tm, d), F32)],
        compiler_params=_cparams(2), name="swiglu_precise" if precise else "swiglu",
    )(x, h, wg, wu, wd, gn)


def _moe_ffn_kernel(te_ref, xs_ref, wg_ref, wu_ref, wd_ref, ys_ref, hb_scr, acc_scr):
    i = pl.program_id(0)
    k = pl.program_id(1)
    nk = pl.num_programs(1)
    n_used = te_ref[pl.num_programs(0)]

    @pl.when(i < n_used)
    def _():
        @pl.when(k == 0)
        def _():
            hb_scr[...] = xs_ref[...].astype(BF16)

        part = _swiglu_part(hb_scr[...], wg_ref, wu_ref, wd_ref)

        @pl.when(k == 0)
        def _():
            acc_scr[...] = part

        @pl.when(k > 0)
        def _():
            acc_scr[...] += part

        @pl.when(k == nk - 1)
        def _():
            ys_ref[...] = acc_scr[...]

    @pl.when((i >= n_used) & (k == nk - 1))
    def _():
        ys_ref[...] = jnp.zeros_like(ys_ref)


def _moe_ffn(te, xs, wg, wu, wd):
    s, d = xs.shape
    f = wg.shape[2]
    nk, tf = _ffn_tiles(f)
    tg = TM_FFN
    grid_spec = pltpu.PrefetchScalarGridSpec(
        num_scalar_prefetch=1, grid=(s // tg, nk),
        in_specs=[pl.BlockSpec((tg, d), lambda i, k, te: (i, 0)),
                  pl.BlockSpec((None, d, tf), lambda i, k, te: (te[i], 0, k)),
                  pl.BlockSpec((None, d, tf), lambda i, k, te: (te[i], 0, k)),
                  pl.BlockSpec((None, tf, d), lambda i, k, te: (te[i], k, 0))],
        out_specs=pl.BlockSpec((tg, d), lambda i, k, te: (i, 0)),
        scratch_shapes=[pltpu.VMEM((tg, d), BF16), pltpu.VMEM((tg, d), F32)])
    return pl.pallas_call(
        _moe_ffn_kernel, grid_spec=grid_spec,
        out_shape=jax.ShapeDtypeStruct((s, d), F32),
        compiler_params=_cparams(2), name="moe_swiglu",
    )(te, xs, wg, wu, wd)


def _head_sum(x, e_ref, et_ref, precise=False):
    e = e_ref[...]
    et = et_ref[...]
    s = _add_all([_dot(p, e) for p in _split(x, 3 if precise else 1)])
    return _add_all([_dot(p, et) for p in _split(s, 3)])


def _softplus(z):
    return jnp.maximum(z, 0.0) + jnp.log1p(jnp.exp(-jnp.abs(z)))


def _rwkv_proj_kernel(*refs, vres, precise, seq_len, n_shift_tiles):
    if vres:
        (h_ref, h8_ref, hps_ref, mu_ref, wrkv_ref, w0_ref, w1_ref, w2_ref, a0_ref, a1_ref, a2_ref, g1_ref, g2_ref,
         kk_ref_, ka_ref, e_ref, et_ref, v0_ref, v1_ref, v2_ref, vf_ref,
         r_o, w_o, k_o, v_o, kk_o, kka_o, g_o) = refs
    else:
        (h_ref, h8_ref, hps_ref, mu_ref, wrkv_ref, w0_ref, w1_ref, w2_ref, a0_ref, a1_ref, a2_ref, g1_ref, g2_ref,
         kk_ref_, ka_ref, e_ref, et_ref,
         r_o, w_o, k_o, v_o, kk_o, kka_o, g_o) = refs
    i = pl.program_id(0)
    tm = h_ref.shape[0]
    h = h_ref[...]
    before = jnp.where((i * tm) % seq_len == 0, 0.0, h8_ref[h8_ref.shape[0] - 1:, :])
    shifted = jnp.concatenate([before, h[:tm - 1]], axis=0)
    hprev = jnp.where(i < n_shift_tiles, shifted, hps_ref[...])
    xx = hprev - h

    mm = functools.partial(_mm, precise=precise)

    def mix(s):
        return (h + xx * mu_ref[s:s + 1, :]).astype(_wdtype(precise))

    r_o[...] = mm(mix(0), wrkv_ref[0])
    k = mm(mix(1), wrkv_ref[1])
    xv = mix(2)
    v = mm(xv, wrkv_ref[2])
    wl = w0_ref[...] + mm(jnp.tanh(mm(mix(3), w1_ref[...])), w2_ref[...])
    w_log = -_softplus(-wl) - 0.5
    w_o[...] = jnp.exp(-jnp.exp(w_log))
    alpha = jax.nn.sigmoid(a0_ref[...] + mm(mm(mix(4), a1_ref[...]), a2_ref[...]))
    if vres:
        gate = jax.nn.sigmoid(v0_ref[...] + mm(mm(xv, v1_ref[...]), v2_ref[...]))
        v = v + (vf_ref[...] - v) * gate
    v_o[...] = v
    g_o[...] = mm(jax.nn.sigmoid(mm(mix(5), g1_ref[...])), g2_ref[...])
    kk = k * kk_ref_[...]
    nrm = jnp.maximum(jnp.sqrt(_head_sum(kk * kk, e_ref, et_ref, precise)), 1e-12)
    kk = kk / nrm
    kk_o[...] = kk
    kka_o[...] = kk * alpha
    k_o[...] = k * (1.0 + (alpha - 1.0) * ka_ref[...])


def _rwkv_proj(h, hps, n_shift_rows, seq_len, mu, wrkv, w0, w1, w2, a0, a1, a2, g1, g2, k_k, k_a, e, et,
               vres_args, precise=False):
    m, d = h.shape
    tm = TM_RWKV
    sub = 8
    assert seq_len % tm == 0 and n_shift_rows % tm == 0
    n_shift_tiles = n_shift_rows // tm
    vres = vres_args is not None
    args = [h, h, hps, mu, wrkv, w0, w1, w2, a0, a1, a2, g1, g2, k_k, k_a, e, et]
    specs = [_rows(tm, d),
             pl.BlockSpec((sub, d), lambda i: (jnp.maximum(i * (tm // sub) - 1, 0), 0)),
             pl.BlockSpec((tm, d), lambda i: (jnp.maximum(i - n_shift_tiles, 0), 0))]
    specs += [_full(a.shape, precise) for a in args[3:]]
    if vres:
        v0, v1, v2, vf = vres_args
        args += [v0, v1, v2, vf]
        specs += [_full(v0.shape), _full(v1.shape), _full(v2.shape), _rows(tm, d)]
    return pl.pallas_call(
        functools.partial(_rwkv_proj_kernel, vres=vres, precise=precise, seq_len=seq_len,
                          n_shift_tiles=n_shift_tiles), grid=(m // tm,),
        in_specs=specs,
        out_specs=[_rows(tm, d)] * 7,
        out_shape=[jax.ShapeDtypeStruct((m, d), F32)] * 7,
        compiler_params=_cparams(1), name="rwkv_proj_precise" if precise else "rwkv_proj",
    )(*args)


N_SCAN_OPS = 6


def _scan_kernel(*refs, nb, nq, prompt, seq_steps, pieces):
    n_in = N_SCAN_OPS * (nb if prompt else 1)
    op_refs = refs[:n_in]
    if prompt:
        y_ref, st_ref = refs[n_in:n_in + 2]
        st_scrs = refs[n_in + 2:]
    else:
        s0_ref, y_ref, st_ref = refs[n_in:n_in + 3]
        st_scrs = refs[n_in + 3:]
    width = 4 * HEAD_DIM
    if prompt:
        @pl.when(pl.program_id(1) == 0)
        def _():
            for s_scr in st_scrs:
                s_scr[...] = jnp.zeros_like(s_scr)
    else:
        for s in range(nb):
            for q in range(nq):
                st_scrs[s * nq + q][...] = s0_ref[s, q]
    sub = lax.broadcasted_iota(jnp.int32, (HEAD_DIM, width), 0)
    lane = lax.broadcasted_iota(jnp.int32, (HEAD_DIM, width), 1)
    dmask = sub == (lane % HEAD_DIM)
    rr = lax.broadcasted_iota(jnp.int32, (width, width), 0)
    cc = lax.broadcasted_iota(jnp.int32, (width, width), 1)
    ones_blk = ((rr // HEAD_DIM) == (cc // HEAD_DIM)).astype(BF16)
    n_p, n_v, n_q = pieces
    n_lhs = (n_p + n_v) * HEAD_DIM

    def blocks_sum(o, start, n):
        return _add_all([o[start + i * HEAD_DIM:start + (i + 1) * HEAD_DIM] for i in range(n)])

    def y_row(yb):
        rows_q = nq * HEAD_DIM
        yb = _add_all([yb[i * rows_q:(i + 1) * rows_q] for i in range(n_q)])
        return jnp.concatenate(
            [jnp.sum(jnp.where(dmask, yb[q * HEAD_DIM:(q + 1) * HEAD_DIM], 0.0), axis=0, keepdims=True)
             for q in range(nq)], axis=1)

    def run_group(row, n_steps):
        ys = [[None] * n_steps for _ in range(nb)]
        q_prev = [None] * nb
        for u in range(n_steps):
            for s in range(nb):
                st = [st_scrs[s * nq + q][...] for q in range(nq)]
                lhs = []
                for q in range(nq):
                    lhs += _split(st[q] * row(0, s, u, q), n_p)
                    lhs += _split(jnp.where(dmask, row(5, s, u, q), 0.0), n_v)
                if u > 0:
                    lhs.append(q_prev[s])
                o = _dot(jnp.concatenate(lhs, axis=0), ones_blk)
                qs = [[] for _ in range(n_q)]
                for q in range(nq):
                    skk = blocks_sum(o, q * n_lhs, n_p)
                    vb = blocks_sum(o, q * n_lhs + n_p * HEAD_DIM, n_v)
                    sn = st[q] * row(1, s, u, q) - skk * row(2, s, u, q) + vb * row(3, s, u, q)
                    st_scrs[s * nq + q][...] = sn
                    for i, piece in enumerate(_split(sn * row(4, s, u, q), n_q)):
                        qs[i].append(piece)
                if u > 0:
                    ys[s][u - 1] = y_row(o[nq * n_lhs:])
                q_prev[s] = jnp.concatenate([p for grp in qs for p in grp], axis=0)
        for s in range(nb):
            ys[s][n_steps - 1] = y_row(_dot(q_prev[s], ones_blk))
        return ys

    if prompt:
        def group(g, c):
            r0 = pl.multiple_of(g * SCAN_GROUP, SCAN_GROUP)
            x = [[op_refs[o * nb + s][pl.ds(r0, SCAN_GROUP), :] for s in range(nb)] for o in range(N_SCAN_OPS)]
            ys = run_group(lambda o, s, u, q: x[o][s][u:u + 1, q * width:(q + 1) * width], SCAN_GROUP)
            for s in range(nb):
                y_ref[s, pl.ds(r0, SCAN_GROUP), :] = jnp.concatenate(ys[s], axis=0)
            return c

        lax.fori_loop(0, op_refs[0].shape[0] // SCAN_GROUP, group, 0)
    else:
        x = [op_refs[o][...] for o in range(N_SCAN_OPS)]
        ys = run_group(lambda o, s, u, q: x[o][s * seq_steps + u:s * seq_steps + u + 1, q * width:(q + 1) * width],
                       seq_steps)
        y_ref[...] = jnp.concatenate([y for s in range(nb) for y in ys[s]], axis=0)

    def write_state():
        for s in range(nb):
            for q in range(nq):
                st_ref[s, q] = st_scrs[s * nq + q][...]

    if prompt:
        pl.when(pl.program_id(1) == pl.num_programs(1) - 1)(write_state)
    else:
        write_state()


def _scan_prompt(ops, n_seq, t_len):
    d = ops[0].shape[1]
    nb = SCAN_SEQS
    nq = d // (4 * HEAD_DIM)
    tb = min(t_len, SCAN_STEPS)
    n_chunks = t_len // tb
    in_specs, args = [], []
    for a in ops:
        for s in range(nb):
            in_specs.append(pl.BlockSpec((tb, d), lambda b, c, s=s: ((b * nb + s) * n_chunks + c, 0)))
            args.append(a)
    st_spec = pl.BlockSpec((nb, nq, HEAD_DIM, 4 * HEAD_DIM), lambda b, c: (b, 0, 0, 0))
    return pl.pallas_call(
        functools.partial(_scan_kernel, nb=nb, nq=nq, prompt=True, seq_steps=t_len, pieces=SCAN_PIECES),
        grid=(n_seq // nb, n_chunks),
        in_specs=in_specs,
        out_specs=[pl.BlockSpec((nb, tb, d), lambda b, c: (b, c, 0)), st_spec],
        out_shape=[jax.ShapeDtypeStruct((n_seq, t_len, d), F32),
                   jax.ShapeDtypeStruct((n_seq, nq, HEAD_DIM, 4 * HEAD_DIM), F32)],
        scratch_shapes=[pltpu.VMEM((HEAD_DIM, 4 * HEAD_DIM), F32)] * (nb * nq),
        compiler_params=_cparams(2), name="wkv_scan_prompt",
    )(*args)


def _scan_sample(ops, s0, row0, n_seq, t_len, precise=False):
    d = ops[0].shape[1]
    nq = d // (4 * HEAD_DIM)
    nb = SCAN_SAMPLE_SEQS
    rows = nb * t_len
    assert rows % SCAN_GROUP == 0 and row0 % rows == 0 and n_seq % nb == 0
    blk0 = row0 // rows
    row_spec = pl.BlockSpec((rows, d), lambda b: (blk0 + b, 0))
    st_spec = pl.BlockSpec((nb, nq, HEAD_DIM, 4 * HEAD_DIM), lambda b: (b, 0, 0, 0))
    return pl.pallas_call(
        functools.partial(_scan_kernel, nb=nb, nq=nq, prompt=False, seq_steps=t_len,
                          pieces=SCAN_PIECES_PRECISE if precise else SCAN_PIECES),
        grid=(n_seq // nb,),
        in_specs=[row_spec] * N_SCAN_OPS + [st_spec],
        out_specs=[pl.BlockSpec((rows, d), lambda b: (b, 0)), st_spec],
        out_shape=[jax.ShapeDtypeStruct((n_seq * t_len, d), F32),
                   jax.ShapeDtypeStruct((n_seq, nq, HEAD_DIM, 4 * HEAD_DIM), F32)],
        scratch_shapes=[pltpu.VMEM((HEAD_DIM, 4 * HEAD_DIM), F32)] * (nb * nq),
        compiler_params=_cparams(1), name="wkv_scan_sample",
    )(*ops, s0)


def _rwkv_out_kernel(x_ref, y_ref, r_ref, k_ref, v_ref, g_ref, rk_ref, lg_ref, lb_ref, e_ref, et_ref,
                     wout_ref, gn_ref, xn_ref, hn_ref, *, precise):
    inv_n = 1.0 / HEAD_DIM
    hsum = functools.partial(_head_sum, e_ref=e_ref, et_ref=et_ref, precise=precise)
    y = y_ref[...]
    yc = y - hsum(y) * inv_n
    var = hsum(yc * yc) * inv_n
    yn = yc * lax.rsqrt(var + GN_EPS) * lg_ref[...] + lb_ref[...]
    bonus = hsum(r_ref[...] * k_ref[...] * rk_ref[...]) * v_ref[...]
    o = _mm((yn + bonus) * g_ref[...], wout_ref[...], precise)
    xn = x_ref[...] + o
    xn_ref[...] = xn
    hn_ref[...] = _rms(xn, gn_ref[...])


def _rwkv_out(x, y, r, k, v, g, rk, lg, lb, e, et, wout, gn, precise=False):
    m, d = x.shape
    tm = TM_RWKV
    small = [rk, lg, lb, e, et, wout, gn]
    return pl.pallas_call(
        functools.partial(_rwkv_out_kernel, precise=precise), grid=(m // tm,),
        in_specs=[_rows(tm, d)] * 6 + [_full(a.shape) for a in small],
        out_specs=[_rows(tm, d)] * 2,
        out_shape=[jax.ShapeDtypeStruct((m, d), F32)] * 2,
        compiler_params=_cparams(1), name="rwkv_out_precise" if precise else "rwkv_out",
    )(x, y, r, k, v, g, *small)


def _router_kernel(h_ref, wr_ref, idx_ref, wts_ref, cnt_ref, run_scr):
    tm = h_ref.shape[0]

    @pl.when(pl.program_id(0) == 0)
    def _():
        run_scr[...] = jnp.zeros_like(run_scr)

    lane = lax.broadcasted_iota(jnp.int32, (tm, LANES), 1)
    lanef = lane.astype(F32)
    logits = jnp.dot(h_ref[...], wr_ref[...], preferred_element_type=F32, precision=lax.Precision.HIGHEST)
    neg = jnp.float32(-jnp.inf)
    logits = jnp.where(lane < N_EXPERTS, logits, neg)
    m1 = jnp.max(logits, axis=-1, keepdims=True)
    i1 = jnp.min(jnp.where(logits == m1, lanef, float(LANES)), axis=-1, keepdims=True)
    l2 = jnp.where(lanef == i1, neg, logits)
    m2 = jnp.max(l2, axis=-1, keepdims=True)
    i2 = jnp.min(jnp.where(l2 == m2, lanef, float(LANES)), axis=-1, keepdims=True)
    ex = jnp.exp(m2 - m1)
    den = 1.0 + ex
    w1 = 1.0 / den
    w2 = ex / den
    oh1 = lanef == i1
    oh2 = lanef == i2
    onehot = jnp.where(oh1 | oh2, 1.0, 0.0)
    ri = lax.broadcasted_iota(jnp.int32, (tm, tm), 0)
    ci = lax.broadcasted_iota(jnp.int32, (tm, tm), 1)
    lower = (ci < ri).astype(BF16)
    cum = _dot(lower, onehot.astype(BF16)) + run_scr[0:1, :]
    rank1 = jnp.sum(jnp.where(oh1, cum, 0.0), axis=-1, keepdims=True)
    rank2 = jnp.sum(jnp.where(oh2, cum, 0.0), axis=-1, keepdims=True)
    run = run_scr[0:1, :] + jnp.sum(onehot, axis=0, keepdims=True)
    run_scr[...] = jnp.broadcast_to(run, run_scr.shape)
    cnt_ref[0] = jnp.broadcast_to(run, cnt_ref.shape[1:]).astype(jnp.int32)
    packed = jnp.where(lane == 0, i1, jnp.where(lane == 1, i2, jnp.where(lane == 2, rank1, jnp.where(lane == 3, rank2, 0.0))))
    idx_ref[...] = packed.astype(jnp.int32)
    wts_ref[...] = jnp.where(lane == 0, w1, jnp.where(lane == 1, w2, 0.0))


def _router(h, wr):
    m, d = h.shape
    tm = TM_ROUTE
    nt = m // tm
    return pl.pallas_call(
        _router_kernel, grid=(nt,),
        in_specs=[_rows(tm, d), _full((d, LANES))],
        out_specs=[_rows(tm, LANES), _rows(tm, LANES), pl.BlockSpec((1, 8, LANES), lambda i: (i, 0, 0))],
        out_shape=[jax.ShapeDtypeStruct((m, LANES), jnp.int32), jax.ShapeDtypeStruct((m, LANES), F32),
                   jax.ShapeDtypeStruct((nt, 8, LANES), jnp.int32)],
        scratch_shapes=[pltpu.VMEM((8, LANES), F32)],
        compiler_params=_cparams(1), name="router",
    )(h, wr)


def _row_copy(src, src_row, dst, dst_row, sem):
    return pltpu.make_async_copy(src.at[pl.ds(src_row, 1)], dst.at[pl.ds(dst_row, 1)], sem)


def _dispatch_kernel(slots_ref, h_ref, xs_in, xs_hbm, sem):
    del xs_in
    tm = h_ref.shape[0]

    def issue(r, c):
        _row_copy(h_ref, r, xs_hbm, slots_ref[0, 0, r], sem).start()
        _row_copy(h_ref, r, xs_hbm, slots_ref[0, 0, tm + r], sem).start()
        return c

    lax.fori_loop(0, tm, issue, 0, unroll=8)

    def drain(r, c):
        _row_copy(h_ref, 0, xs_hbm, 0, sem).wait()
        _row_copy(h_ref, 0, xs_hbm, 0, sem).wait()
        return c

    lax.fori_loop(0, tm, drain, 0)


def _dispatch(slots, h, n_slots):
    m, d = h.shape
    nt = slots.shape[0]
    xs0 = jnp.zeros((n_slots, d), F32)
    return pl.pallas_call(
        _dispatch_kernel, grid=(nt,),
        in_specs=[pl.BlockSpec((1, 1, slots.shape[2]), lambda i: (i, 0, 0), memory_space=pltpu.SMEM),
                  _rows(m // nt, d), pl.BlockSpec(memory_space=pl.ANY)],
        out_specs=pl.BlockSpec(memory_space=pl.ANY),
        out_shape=jax.ShapeDtypeStruct((n_slots, d), F32),
        scratch_shapes=[pltpu.SemaphoreType.DMA(())],
        input_output_aliases={2: 0},
        compiler_params=_cparams(1), name="moe_dispatch",
    )(slots, h, xs0)


def _combine_kernel(slots_ref, x_ref, wts_ref, ys_hbm, gn_ref, xn_ref, hn_ref, buf, sem):
    tm = x_ref.shape[0]

    def issue(r, c):
        _row_copy(ys_hbm, slots_ref[0, 0, r], buf.at[0], r, sem).start()
        _row_copy(ys_hbm, slots_ref[0, 0, tm + r], buf.at[1], r, sem).start()
        return c

    lax.fori_loop(0, tm, issue, 0, unroll=8)

    def drain(r, c):
        _row_copy(ys_hbm, 0, buf.at[0], 0, sem).wait()
        _row_copy(ys_hbm, 0, buf.at[1], 0, sem).wait()
        return c

    lax.fori_loop(0, tm, drain, 0)
    wts = wts_ref[...]
    xn = x_ref[...] + wts[:, 0:1] * buf[0] + wts[:, 1:2] * buf[1]
    xn_ref[...] = xn
    hn_ref[...] = _rms(xn, gn_ref[...])


def _combine(slots, x, wts, ys, gn):
    m, d = x.shape
    tm = TM_ROUTE
    return pl.pallas_call(
        _combine_kernel, grid=(m // tm,),
        in_specs=[pl.BlockSpec((1, 1, slots.shape[2]), lambda i: (i, 0, 0), memory_space=pltpu.SMEM),
                  _rows(tm, d), _rows(tm, LANES), pl.BlockSpec(memory_space=pl.ANY), _full((1, d))],
        out_specs=[_rows(tm, d)] * 2,
        out_shape=[jax.ShapeDtypeStruct((m, d), F32)] * 2,
        scratch_shapes=[pltpu.VMEM((2, tm, d), F32), pltpu.SemaphoreType.DMA(())],
        compiler_params=_cparams(1), name="moe_combine",
    )(slots, x, wts, ys, gn)


def _moe(x, h, router, wg, wu, wd, gn):
    m, d = x.shape
    tg = TM_FFN
    tm = TM_ROUTE
    wr = jnp.zeros((d, LANES), F32).at[:, :N_EXPERTS].set(router)
    idx, wts, cnt = _router(h, wr)
    counts = cnt[-1, 0, :N_EXPERTS]
    padded = ((counts + tg - 1) // tg) * tg
    ends = jnp.cumsum(padded)
    offs = ends - padded
    n_tiles = (2 * m) // tg + N_EXPERTS
    expert_ids = jnp.arange(N_EXPERTS, dtype=jnp.int32)[None, :]
    slot1 = jnp.sum(jnp.where(idx[:, 0:1] == expert_ids, offs[None, :], 0), axis=1) + idx[:, 2]
    slot2 = jnp.sum(jnp.where(idx[:, 1:2] == expert_ids, offs[None, :], 0), axis=1) + idx[:, 3]
    slots = jnp.concatenate([slot1.reshape(m // tm, 1, tm), slot2.reshape(m // tm, 1, tm)], axis=2).astype(jnp.int32)
    starts = jnp.arange(n_tiles, dtype=jnp.int32) * tg
    tile_expert = jnp.minimum(jnp.sum(ends[None, :] <= starts[:, None], axis=1), N_EXPERTS - 1)
    te = jnp.concatenate([tile_expert, ends[-1:] // tg]).astype(jnp.int32)
    xs = _dispatch(slots, h, n_tiles * tg)
    ys = _moe_ffn(te, xs, wg, wu, wd)
    return _combine(slots, x, wts, ys, gn)


def _pad_lora(a, b):
    rnk = a.shape[1]
    rp = -(-rnk // LANES) * LANES
    return jnp.pad(a, ((0, 0), (0, rp - rnk))), jnp.pad(b, ((0, rp - rnk), (0, 0)))


def kernel(x_prompt, x_sample, state_shift, state_wkv, norm_mix_g, norm_ffn_g, norm_out_g, a_w_in, a_b_in, a_ln_g, a_ln_b, a_w_s, a_b_s, a_w_out, b_mu, b_w_rkv, b_w0, b_w1, b_w2, b_a0, b_a1, b_a2, b_v0, b_v1, b_v2, b_g1, b_g2, b_k_k, b_k_a, b_r_k, b_lnx_g, b_lnx_b, b_w_out, f_w_gate, f_w_up, f_w_down, m_router, m_w_gate, m_w_up, m_w_down):
    bp, tp, d = x_prompt.shape
    bs, ts, _ = x_sample.shape
    depth = norm_mix_g.shape[0]
    n_heads = d // HEAD_DIM
    mp = bp * tp
    ms = bs * ts
    aw = a_w_out.shape[1]
    gd = aw // A_GROUPS
    row = lambda a: a.reshape(1, -1)

    nq = n_heads // 4
    next_gain = lambda i, sub: row(norm_ffn_g[i]) if sub == 0 else row(norm_mix_g[i + 1] if i + 1 < depth else norm_out_g)

    head_of = jnp.arange(d) // HEAD_DIM
    e_mat = (head_of[:, None] == jnp.arange(LANES)[None, :]).astype(BF16)
    et_mat = e_mat.T

    def layer_a(x, h, i, n_prompt_rows, precise):
        j = i // 2
        wdt = _wdtype(precise)
        w_in = a_w_in[j].astype(wdt)
        tri = jnp.tril(jnp.ones((CHUNK, CHUNK), bool))
        wc_p = jnp.where(tri, a_w_s[j], 0.0)
        rr = jnp.arange(CHUNK)
        same_seq = (rr[:, None] // ts) == (rr[None, :] // ts)
        reps = CHUNK // ts
        wc_s = jnp.where(same_seq & tri, jnp.tile(a_w_s[j][:, :ts, :ts], (1, reps, reps)), 0.0)
        wc2 = jnp.stack([wc_p, wc_s]).astype(wdt)
        bs_p = jnp.repeat(a_b_s[j].T, gd, axis=1)
        bs_s = jnp.repeat(jnp.tile(a_b_s[j][:, :ts], (1, reps)).T, gd, axis=1)
        bs2 = jnp.stack([bs_p, bs_s])
        x, h, vn = _sgu(x, h, w_in[:, :aw], w_in[:, aw:], row(a_b_in[j][:aw]), row(a_b_in[j][aw:]),
                        row(a_ln_g[j]), row(a_ln_b[j]), wc2, bs2, a_w_out[j].astype(wdt),
                        next_gain(i, 0), n_prompt_rows, precise)
        x, h = _ffn(x, h, f_w_gate[j].astype(wdt), f_w_up[j].astype(wdt), f_w_down[j].astype(wdt),
                    next_gain(i, 1), precise)
        return x, h, vn

    def tile_state(s):
        n = s.shape[0]
        return s.reshape(n, nq, 4, HEAD_DIM, HEAD_DIM).transpose(0, 1, 3, 2, 4).reshape(n, nq, HEAD_DIM, 4 * HEAD_DIM)

    def untile_state(s):
        n = s.shape[0]
        return s.reshape(n, nq, HEAD_DIM, 4, HEAD_DIM).transpose(0, 1, 3, 2, 4).reshape(n, n_heads, HEAD_DIM, HEAD_DIM)

    def rwkv_mixer(x, h, i, n_p_rows, n_s_rows, v_first, precise):
        j = i // 2
        wdt = _wdtype(precise)
        shifts = {}
        if n_p_rows:
            shifts["p"] = h[tp - 1:n_p_rows:tp]
        if n_s_rows:
            hs3 = h[n_p_rows:].reshape(bs, ts, d)
            shifts["s"] = hs3[:, -1]
            hps = jnp.concatenate([state_shift[j][:, None, :], hs3[:, :-1]], axis=1).reshape(n_s_rows, d)
        else:
            hps = jnp.zeros((TM_RWKV, d), F32)
        lora = lambda a, b: tuple(z.astype(wdt) for z in _pad_lora(a, b))
        w1, w2 = lora(b_w1[j], b_w2[j])
        a1, a2 = lora(b_a1[j], b_a2[j])
        g1, g2 = lora(b_g1[j], b_g2[j])
        vres_args = None
        if j > 0:
            v1, v2 = lora(b_v1[j - 1], b_v2[j - 1])
            vres_args = (row(b_v0[j - 1]), v1, v2, v_first)
        r, w, k, v, kk, kka, g = _rwkv_proj(
            h, hps, n_p_rows, tp, b_mu[j], b_w_rkv[j].astype(wdt), row(b_w0[j]), w1, w2, row(b_a0[j]), a1, a2,
            g1, g2, row(b_k_k[j]), row(b_k_a[j]), e_mat, et_mat, vres_args, precise)
        ops = [kk, w, kka, k, r, v]
        ys, states = [], {}
        if n_p_rows:
            y_p, st_p = _scan_prompt(ops, bp, tp)
            ys.append(y_p.reshape(n_p_rows, d))
            states["p"] = untile_state(st_p)
        if n_s_rows:
            y_s, st_s = _scan_sample(ops, tile_state(state_wkv[j]), n_p_rows, bs, ts, precise)
            ys.append(y_s)
            states["s"] = untile_state(st_s)
        y = ys[0] if len(ys) == 1 else jnp.concatenate(ys, axis=0)
        x, h = _rwkv_out(x, y, r, k, v, g, row(b_r_k[j].reshape(-1)), row(b_lnx_g[j]), row(b_lnx_b[j]),
                         e_mat, et_mat, b_w_out[j].astype(wdt), next_gain(i, 0), precise)
        return x, h, v, shifts, states

    chunk_v, shift_p, wkv_p, shift_s, wkv_s = [], [], [], [], []
    xp = x_prompt.reshape(mp, d)
    xs = x_sample.reshape(ms, d)
    xp, hp, _ = layer_a(xp, row(norm_mix_g[0]), 0, mp, False)
    xs, hs, vn_s = layer_a(xs, row(norm_mix_g[0]), 0, 0, True)
    chunk_v.append(vn_s.reshape(bs, ts, aw))
    v_first = None
    if depth > 1:
        xp, hp, v_p, sh, st = rwkv_mixer(xp, hp, 1, mp, 0, None, False)
        shift_p.append(sh["p"])
        wkv_p.append(st["p"])
        xs, hs, v_s, sh, st = rwkv_mixer(xs, hs, 1, 0, ms, None, True)
        shift_s.append(sh["s"])
        wkv_s.append(st["s"])
        v_first = jnp.concatenate([v_p, v_s], axis=0)
    x = jnp.concatenate([xp, xs], axis=0)
    h = jnp.concatenate([hp, hs], axis=0)
    for i in range(1, depth):
        j = i // 2
        if i % 2 == 0:
            x, h, vn = layer_a(x, h, i, mp, False)
            chunk_v.append(vn[mp:].reshape(bs, ts, aw))
        else:
            if i > 1:
                x, h, _, sh, st = rwkv_mixer(x, h, i, mp, ms, v_first, False)
                shift_p.append(sh["p"])
                shift_s.append(sh["s"])
                wkv_p.append(st["p"])
                wkv_s.append(st["s"])
            x, h = _moe(x, h, m_router[j], m_w_gate[j].astype(BF16), m_w_up[j].astype(BF16),
                        m_w_down[j].astype(BF16), next_gain(i, 1))
    y_prompt = h[:mp].reshape(bp, tp, d)
    y_sample = h[mp:].reshape(bs, ts, d)
    return (y_prompt, y_sample, jnp.stack(chunk_v), jnp.stack(shift_p), jnp.stack(wkv_p),
            jnp.stack(shift_s), jnp.stack(wkv_s))
```
